```python
import math
import jax, jax.numpy as jnp
from jax import lax
import numpy as np

D_MODEL = 4096
BATCH = 1
SEQ = 8192
DEPTH = 1

HEAD_DIM = 128
DILATION_GROUPS = ((128, 1), (512, 4), (2048, 16))
A_HEADS_PER_GROUP = 6
N_HEADS_A = A_HEADS_PER_GROUP * len(DILATION_GROUPS)
N_HEADS_B = D_MODEL // HEAD_DIM - N_HEADS_A
WIDTH_A = N_HEADS_A * HEAD_DIM
WIDTH_B = N_HEADS_B * HEAD_DIM
WIDTH_A_OUT = A_HEADS_PER_GROUP * HEAD_DIM
IN_COLS = 3 * WIDTH_A + 3 * WIDTH_B + N_HEADS_B + 2 * D_MODEL
Q_BLOCK = 128
ALIBI_MAX_EXP = 8.0
PEER_HEADS = 8
N_KEYS = 128
N_EXPERTS = N_KEYS * N_KEYS
PEER_TOPK = 16
PEER_QDIM = 256
PEER_QHALF = PEER_QDIM // 2
PEER_CHUNK = 64
EPS = 1e-6
NEG = -1e30

kernel_name = "hybrid_dilated_fox_peer_block"


def _rmsnorm(x, gain):
    x32 = x.astype(jnp.float32)
    y = x32 * lax.rsqrt(jnp.mean(x32 * x32, axis=-1, keepdims=True) + EPS)
    return (y * gain.astype(jnp.float32)).astype(x.dtype)


def _alibi_slopes(n):
    return jnp.exp2(-ALIBI_MAX_EXP * jnp.arange(1, n + 1, dtype=jnp.float32) / n)


def _dilated_group(q, k, v, slopes, window, dilation):
    B, S, H, Dh = q.shape
    L = window // dilation
    span = dilation * L
    S_pad = -(-S // span) * span
    M = S_pad // dilation
    nb = M // L

    def to_blocks(t):
        t = jnp.pad(t, ((0, 0), (0, S_pad - S), (0, 0), (0, 0)))
        t = t.reshape(B, M, dilation, H, Dh).transpose(0, 2, 1, 3, 4)
        return t.reshape(B, dilation, nb, L, H, Dh)

    def with_prev(t):
        prev = jnp.pad(t[:, :, :-1], ((0, 0), (0, 0), (1, 0), (0, 0), (0, 0), (0, 0)))
        return jnp.concatenate([prev, t], axis=3)

    qb = to_blocks(q)
    kc = with_prev(to_blocks(k))
    vc = with_prev(to_blocks(v))
    s = jnp.einsum('bcnqhd,bcnkhd->bcnqhk', qb, kc).astype(jnp.float32) * (HEAD_DIM ** -0.5)
    qi = jnp.arange(L)[:, None]
    kj = jnp.arange(2 * L)[None, :]
    delta = qi + L - kj
    band = (delta >= 0) & (delta <= L)
    first = (jnp.arange(nb)[:, None, None] == 0) & (kj[None] < L)
    valid = band[None] & (~first)
    bias = -(slopes * dilation)[None, :, None] * delta[:, None, :].astype(jnp.float32)
    s = jnp.where(valid[:, :, None, :], s + bias, NEG)
    m = jnp.max(s, axis=-1)
    p = jnp.exp(s - m[..., None])
    l = jnp.sum(p, axis=-1)
    o = jnp.einsum('bcnqhk,bcnkhd->bcnqhd', p, vc.astype(jnp.float32))

    def from_blocks(t):
        t = t.reshape((B, dilation, M) + t.shape[4:])
        t = jnp.swapaxes(t, 1, 2)
        return t.reshape((B, S_pad) + t.shape[3:])[:, :S]

    return from_blocks(o), from_blocks(m), from_blocks(l)


def _dilated_mixture(q, k, v):
    slopes = _alibi_slopes(N_HEADS_A)
    res = []
    for g, (window, dilation) in enumerate(DILATION_GROUPS):
        sl = slice(g * A_HEADS_PER_GROUP, (g + 1) * A_HEADS_PER_GROUP)
        res.append(_dilated_group(q[:, :, sl], k[:, :, sl], v[:, :, sl], slopes[sl], window, dilation))
    m_star = jnp.max(jnp.stack([r[1] for r in res], 0), axis=0)
    num = 0.0
    den = 0.0
    for o, m, l in res:
        w = jnp.exp(m - m_star)
        num = num + w[..., None] * o
        den = den + w * l
    return (num / den[..., None]).astype(q.dtype)


def _forgetting_attention(q, k, v, logf):
    B, S, H, Dh = q.shape
    c = jnp.cumsum(logf.astype(jnp.float32), axis=1).transpose(0, 2, 1)
    kpos = jnp.arange(S)
    nb = S // Q_BLOCK

    def block(i):
        start = i * Q_BLOCK
        qb = lax.dynamic_slice_in_dim(q, start, Q_BLOCK, axis=1)
        cq = lax.dynamic_slice_in_dim(c, start, Q_BLOCK, axis=2)
        s = jnp.einsum('bqhd,bkhd->bhqk', qb, k).astype(jnp.float32) * (HEAD_DIM ** -0.5)
        s = s + cq[..., :, None] - c[:, :, None, :]
        qpos = start + jnp.arange(Q_BLOCK)
        s = jnp.where(kpos[None, :] <= qpos[:, None], s, NEG)
        p = jax.nn.softmax(s, axis=-1)
        return jnp.einsum('bhqk,bkhd->bqhd', p.astype(v.dtype), v)

    o = lax.map(block, jnp.arange(nb))
    return o.transpose(1, 0, 2, 3, 4).reshape(B, S, H, Dh)


def _peer(h, w_q, subkeys, u, v):
    B, S, D = h.shape
    q = (h @ w_q).reshape(B, S, PEER_HEADS, 2, PEER_QHALF)
    sc = jnp.einsum('bshcd,hcnd->bshcn', q, subkeys).astype(jnp.float32)
    s1, i1 = lax.top_k(sc[..., 0, :], PEER_TOPK)
    s2, i2 = lax.top_k(sc[..., 1, :], PEER_TOPK)
    cand = (s1[..., :, None] + s2[..., None, :]).reshape(B, S, PEER_HEADS, PEER_TOPK * PEER_TOPK)
    cidx = (i1[..., :, None] * N_KEYS + i2[..., None, :]).reshape(B, S, PEER_HEADS, PEER_TOPK * PEER_TOPK)
    top, pos = lax.top_k(cand, PEER_TOPK)
    eidx = jnp.take_along_axis(cidx, pos, axis=-1)
    gate = jax.nn.softmax(top, axis=-1)
    T = B * S
    K = PEER_HEADS * PEER_TOPK
    nc = T // PEER_CHUNK
    hc = h.reshape(nc, PEER_CHUNK, D)
    ec = eidx.reshape(nc, PEER_CHUNK, K)
    gc = gate.reshape(nc, PEER_CHUNK, K).astype(h.dtype)

    def chunk(args):
        hx, ex, gx = args
        a = jnp.einsum('ckd,cd->ck', u[ex], hx)
        act = jax.nn.gelu(a, approximate=False) * gx
        return jnp.einsum('ck,ckd->cd', act, v[ex])

    out = lax.map(chunk, (hc, ec, gc))
    return out.reshape(B, S, D)


def setup_inputs(seed: int = 0) -> dict:
    key = jax.random.key(seed)
    ks = jax.random.split(key, 20)
    f32 = jnp.float32
    nrm = lambda k, shape, scale: jax.random.normal(k, shape, f32) * scale
    return {
        "x": nrm(ks[0], (BATCH, SEQ, D_MODEL), 1.0),
        "norm1_gain": 1.0 + nrm(ks[1], (DEPTH, D_MODEL), 0.02),
        "w_in": nrm(ks[2], (DEPTH, D_MODEL, IN_COLS), D_MODEL ** -0.5),
        "b_forget": 3.0 + nrm(ks[3], (DEPTH, N_HEADS_B), 0.1),
        "q_norm_a": 1.0 + nrm(ks[4], (DEPTH, HEAD_DIM), 0.02),
        "k_norm_a": 1.0 + nrm(ks[5], (DEPTH, HEAD_DIM), 0.02),
        "q_norm_b": 1.0 + nrm(ks[6], (DEPTH, HEAD_DIM), 0.02),
        "k_norm_b": 1.0 + nrm(ks[7], (DEPTH, HEAD_DIM), 0.02),
        "w_up_a": nrm(ks[8], (DEPTH, WIDTH_A_OUT, D_MODEL), WIDTH_A_OUT ** -0.5),
        "w_up_b": nrm(ks[9], (DEPTH, WIDTH_B, D_MODEL), WIDTH_B ** -0.5),
        "w_out": nrm(ks[10], (DEPTH, D_MODEL, D_MODEL), D_MODEL ** -0.5),
        "norm2_gain": 1.0 + nrm(ks[11], (DEPTH, D_MODEL), 0.02),
        "w_peer_q": nrm(ks[12], (DEPTH, D_MODEL, PEER_HEADS * PEER_QDIM), D_MODEL ** -0.5),
        "peer_subkeys": nrm(ks[13], (DEPTH, PEER_HEADS, 2, N_KEYS, PEER_QHALF), PEER_QHALF ** -0.5),
        "peer_u": nrm(ks[14], (DEPTH, N_EXPERTS, D_MODEL), D_MODEL ** -0.5),
        "peer_v": nrm(ks[15], (DEPTH, N_EXPERTS, D_MODEL), D_MODEL ** -0.5),
    }


def reference(x, norm1_gain, w_in, b_forget, q_norm_a, k_norm_a, q_norm_b, k_norm_b,
              w_up_a, w_up_b, w_out, norm2_gain, w_peer_q, peer_subkeys, peer_u, peer_v):
    B, S, D = x.shape
    sizes = [WIDTH_A, WIDTH_A, WIDTH_A, WIDTH_B, WIDTH_B, WIDTH_B, N_HEADS_B, D_MODEL, D_MODEL]
    cuts = [int(c) for c in np.cumsum(sizes)[:-1]]
    h = x
    for layer in range(DEPTH):
        xn = _rmsnorm(h, norm1_gain[layer])
        proj = xn @ w_in[layer]
        qa, ka, va, qb, kb, vb, f_logit, ga, gb = jnp.split(proj, cuts, axis=-1)
        qa = _rmsnorm(qa.reshape(B, S, N_HEADS_A, HEAD_DIM), q_norm_a[layer])
        ka = _rmsnorm(ka.reshape(B, S, N_HEADS_A, HEAD_DIM), k_norm_a[layer])
        va = va.reshape(B, S, N_HEADS_A, HEAD_DIM)
        qb = _rmsnorm(qb.reshape(B, S, N_HEADS_B, HEAD_DIM), q_norm_b[layer])
        kb = _rmsnorm(kb.reshape(B, S, N_HEADS_B, HEAD_DIM), k_norm_b[layer])
        vb = vb.reshape(B, S, N_HEADS_B, HEAD_DIM)
        logf = jax.nn.log_sigmoid(f_logit.astype(jnp.float32) + b_forget[layer].astype(jnp.float32))
        y_a = _dilated_mixture(qa, ka, va).reshape(B, S, WIDTH_A_OUT)
        y_b = _forgetting_attention(qb, kb, vb, logf).reshape(B, S, WIDTH_B)
        merged = jax.nn.sigmoid(ga) * (y_a @ w_up_a[layer]) + jax.nn.sigmoid(gb) * (y_b @ w_up_b[layer])
        h = h + merged @ w_out[layer]
        hn = _rmsnorm(h, norm2_gain[layer])
        h = h + _peer(hn, w_peer_q[layer], peer_subkeys[layer], peer_u[layer], peer_v[layer])
    return h
```

```python
import functools

import numpy as np
import jax
import jax.numpy as jnp
from jax import lax
from jax.experimental import pallas as pl
from jax.experimental.pallas import tpu as pltpu

F32 = jnp.float32
BF16 = jnp.bfloat16

D_MODEL = 4096
HEAD_DIM = 128
DILATION_GROUPS = ((128, 1), (512, 4), (2048, 16))
A_SLOTS = 6
N_HEADS_A = A_SLOTS * len(DILATION_GROUPS)
N_HEADS_B = D_MODEL // HEAD_DIM - N_HEADS_A
WIDTH_A = N_HEADS_A * HEAD_DIM
WIDTH_B = N_HEADS_B * HEAD_DIM
WIDTH_A_OUT = A_SLOTS * HEAD_DIM
QKV_COLS = 3 * WIDTH_A + 3 * WIDTH_B
ALIBI_MAX_EXP = 8.0
PEER_HEADS = 8
N_KEYS = 128
N_EXPERTS = N_KEYS * N_KEYS
PEER_TOPK = 16
PEER_QDIM = 256
EPS = 1e-6
NEG = -1e30
DIL_BLOCK = 128

VMEM_LIMIT_BYTES = 52 * 1024 * 1024

NORM_TM = 512
QKV_TM, QKV_TN = 1024, 512
FORGET_TM = 256
FOX_TQ = 512
MERGE_TM = 512
UP_TM, UP_TN = 512, 512
OUT_TM, OUT_TN = 1024, 512
PQ_TM = 512
TOPK_TL = 256
PEER_TM, PEER_TE = 512, 512
PEER_LC = 128
FINAL_TM = 256


def _cparams(sem, big=False):
    return pltpu.CompilerParams(dimension_semantics=sem, vmem_limit_bytes=VMEM_LIMIT_BYTES if big else None)


def _dot(a, b):
    return jnp.dot(a, b, preferred_element_type=F32)


def _dot_nt(a, b):
    return lax.dot_general(a, b, (((1,), (1,)), ((), ())), preferred_element_type=F32)


def _rmsnorm_kernel(x_ref, g_ref, o_ref):
    x = x_ref[...]
    ms = jnp.mean(x * x, axis=-1, keepdims=True)
    o_ref[...] = (x * lax.rsqrt(ms + EPS) * g_ref[...]).astype(o_ref.dtype)


def _rmsnorm_bf16(x2d, gain):
    s, d = x2d.shape
    return pl.pallas_call(
        _rmsnorm_kernel,
        grid=(s // NORM_TM,),
        in_specs=[pl.BlockSpec((NORM_TM, d), lambda i: (i, 0)), pl.BlockSpec((1, d), lambda i: (0, 0))],
        out_specs=pl.BlockSpec((NORM_TM, d), lambda i: (i, 0)),
        out_shape=jax.ShapeDtypeStruct((s, d), BF16),
        compiler_params=_cparams(("parallel",)),
        name="rmsnorm",
    )(x2d, gain.reshape(1, d))


def _qkv_kernel(a_ref, w_ref, gain_ref, flag_ref, o_ref):
    acc = _dot(a_ref[...], w_ref[...])
    for c in range(QKV_TN // HEAD_DIM):
        sl = slice(c * HEAD_DIM, (c + 1) * HEAD_DIM)
        y = acc[:, sl]
        ms = jnp.mean(y * y, axis=-1, keepdims=True)
        yn = y * lax.rsqrt(ms + EPS) * gain_ref[:, sl]
        o_ref[:, sl] = jnp.where(flag_ref[:, sl] > 0.0, yn, y).astype(o_ref.dtype)


def _qkv_proj(xn, w_qkv, gain, flag):
    s, d = xn.shape
    n = w_qkv.shape[1]
    return pl.pallas_call(
        _qkv_kernel,
        grid=(s // QKV_TM, n // QKV_TN),
        in_specs=[
            pl.BlockSpec((QKV_TM, d), lambda i, j: (i, 0)),
            pl.BlockSpec((d, QKV_TN), lambda i, j: (0, j)),
            pl.BlockSpec((1, QKV_TN), lambda i, j: (0, j)),
            pl.BlockSpec((1, QKV_TN), lambda i, j: (0, j)),
        ],
        out_specs=pl.BlockSpec((QKV_TM, QKV_TN), lambda i, j: (i, j)),
        out_shape=jax.ShapeDtypeStruct((s, n), BF16),
        compiler_params=_cparams(("parallel", "arbitrary"), big=True),
        name="qkv_proj",
    )(xn, w_qkv, gain, flag)


def _forget_kernel(a_ref, w_ref, b_ref, c_ref, ct_ref, carry_ref):
    @pl.when(pl.program_id(0) == 0)
    def _():
        carry_ref[...] = jnp.zeros_like(carry_ref)

    f = _dot(a_ref[...], w_ref[...]) + b_ref[...]
    logf = jnp.minimum(f, 0.0) - jnp.log1p(jnp.exp(-jnp.abs(f)))
    r = lax.broadcasted_iota(jnp.int32, (FORGET_TM, FORGET_TM), 0)
    c = lax.broadcasted_iota(jnp.int32, (FORGET_TM, FORGET_TM), 1)
    tri = (c <= r).astype(F32)
    cs = jnp.dot(tri, logf, precision=lax.Precision.HIGHEST, preferred_element_type=F32) + carry_ref[...]
    c_ref[...] = cs
    ct_ref[...] = cs.T
    carry_ref[...] = cs[FORGET_TM - 1:FORGET_TM, :]


def _forget_cumsum(xn, w_f, b_f):
    s, d = xn.shape
    return pl.pallas_call(
        _forget_kernel,
        grid=(s // FORGET_TM,),
        in_specs=[
            pl.BlockSpec((FORGET_TM, d), lambda i: (i, 0)),
            pl.BlockSpec((d, HEAD_DIM), lambda i: (0, 0)),
            pl.BlockSpec((1, HEAD_DIM), lambda i: (0, 0)),
        ],
        out_specs=[pl.BlockSpec((FORGET_TM, HEAD_DIM), lambda i: (i, 0)), pl.BlockSpec((HEAD_DIM, FORGET_TM), lambda i: (0, i))],
        out_shape=[jax.ShapeDtypeStruct((s, HEAD_DIM), F32), jax.ShapeDtypeStruct((HEAD_DIM, s), F32)],
        scratch_shapes=[pltpu.VMEM((1, HEAD_DIM), F32)],
        compiler_params=_cparams(("arbitrary",)),
        name="forget_cumsum",
    )(xn, w_f, b_f)


def _alibi_slopes():
    n = N_HEADS_A
    return np.exp2(-np.float32(ALIBI_MAX_EXP) * np.arange(1, n + 1, dtype=np.float32) / np.float32(n)).astype(np.float32)


def _dilated_kernel(q_ref, kp_ref, kc_ref, vp_ref, vc_ref, o_ref, m_ref, l_ref, *, step_slopes):
    n = pl.program_id(1)
    L = DIL_BLOCK
    qi = lax.broadcasted_iota(jnp.int32, (L, L), 0)
    kj = lax.broadcasted_iota(jnp.int32, (L, L), 1)
    d_cur = qi - kj
    d_prev = d_cur + L
    valid_cur = d_cur >= 0
    valid_prev = d_prev <= jnp.where(n > 0, L, -1)
    d_cur_f = d_cur.astype(F32)
    d_prev_f = d_prev.astype(F32)
    for j in range(A_SLOTS):
        sl = slice(j * HEAD_DIM, (j + 1) * HEAD_DIM)
        q = q_ref[:, sl]
        slope = float(step_slopes[j])
        s_c = jnp.where(valid_cur, _dot_nt(q, kc_ref[:, sl]) - slope * d_cur_f, NEG)
        s_p = jnp.where(valid_prev, _dot_nt(q, kp_ref[:, sl]) - slope * d_prev_f, NEG)
        m = jnp.maximum(jnp.max(s_c, axis=-1, keepdims=True), jnp.max(s_p, axis=-1, keepdims=True))
        p_c = jnp.exp(s_c - m)
        p_p = jnp.exp(s_p - m)
        l = jnp.sum(p_c, axis=-1, keepdims=True) + jnp.sum(p_p, axis=-1, keepdims=True)
        o_ref[:, sl] = _dot(p_c.astype(BF16), vc_ref[:, sl]) + _dot(p_p.astype(BF16), vp_ref[:, sl])
        m_ref[:, sl] = jnp.broadcast_to(m, (L, HEAD_DIM))
        l_ref[:, sl] = jnp.broadcast_to(l, (L, HEAD_DIM))


def _dilated_group(qkv, g, dilation):
    s = qkv.shape[0]
    r = dilation
    L = DIL_BLOCK
    view = qkv.reshape(s // r, r * QKV_COLS)
    per_row = QKV_COLS // WIDTH_A_OUT
    k_off = WIDTH_A // WIDTH_A_OUT
    v_off = 2 * WIDTH_A // WIDTH_A_OUT
    slopes = _alibi_slopes()[g * A_SLOTS:(g + 1) * A_SLOTS] * np.float32(r)

    def cur(off):
        return pl.BlockSpec((L, WIDTH_A_OUT), lambda c, n: (n, c * per_row + off + g))

    def prev(off):
        return pl.BlockSpec((L, WIDTH_A_OUT), lambda c, n: (jnp.maximum(n - 1, 0), c * per_row + off + g))

    out_spec = pl.BlockSpec((L, WIDTH_A_OUT), lambda c, n: (n, c))
    out_sds = jax.ShapeDtypeStruct((s // r, r * WIDTH_A_OUT), F32)
    o, m, l = pl.pallas_call(
        functools.partial(_dilated_kernel, step_slopes=tuple(float(v) for v in slopes)),
        grid=(r, s // (r * L)),
        in_specs=[cur(0), prev(k_off), cur(k_off), prev(v_off), cur(v_off)],
        out_specs=[out_spec, out_spec, out_spec],
        out_shape=[out_sds, out_sds, out_sds],
        compiler_params=_cparams(("parallel", "arbitrary")),
        name=f"dilated_g{g}",
    )(view, view, view, view, view)
    shape = (s, WIDTH_A_OUT)
    return o.reshape(shape), m.reshape(shape), l.reshape(shape)


def _merge_kernel(*refs):
    n = len(DILATION_GROUPS)
    o_refs, m_refs, l_refs, y_ref = refs[:n], refs[n:2 * n], refs[2 * n:3 * n], refs[3 * n]
    ms = [r[...] for r in m_refs]
    m_star = functools.reduce(jnp.maximum, ms)
    num = 0.0
    den = 0.0
    for o_r, m, l_r in zip(o_refs, ms, l_refs):
        w = jnp.exp(m - m_star)
        num = num + w * o_r[...]
        den = den + w * l_r[...]
    y_ref[...] = (num / den).astype(y_ref.dtype)


def _dilated_mixture(qkv):
    s = qkv.shape[0]
    res = [_dilated_group(qkv, g, dil) for g, (_, dil) in enumerate(DILATION_GROUPS)]
    spec = pl.BlockSpec((MERGE_TM, WIDTH_A_OUT), lambda i: (i, 0))
    args = [r[0] for r in res] + [r[1] for r in res] + [r[2] for r in res]
    return pl.pallas_call(
        _merge_kernel,
        grid=(s // MERGE_TM,),
        in_specs=[spec] * len(args),
        out_specs=spec,
        out_shape=jax.ShapeDtypeStruct((s, WIDTH_A_OUT), BF16),
        compiler_params=_cparams(("parallel",)),
        name="dilated_merge",
    )(*args)


def _fox_kernel(q_ref, k_ref, v_ref, cq_ref, ck_ref, o_ref):
    h = pl.program_id(0)
    qb = pl.program_id(1)
    T = FOX_TQ
    q = q_ref[...]
    lane = lax.broadcasted_iota(jnp.int32, (T, HEAD_DIM), 1)
    cq = jnp.sum(jnp.where(lane == h, cq_ref[...], 0.0), axis=-1, keepdims=True)
    rel = lax.broadcasted_iota(jnp.int32, (T, T), 0) - lax.broadcasted_iota(jnp.int32, (T, T), 1)

    def body(j, carry):
        m, l, acc = carry
        start = pl.multiple_of(j * T, T)
        k = k_ref[pl.ds(start, T), :]
        v = v_ref[pl.ds(start, T), :]
        s = _dot_nt(q, k) + cq - ck_ref[:, pl.ds(start, T)]
        s = jnp.where(rel >= (j - qb) * T, s, NEG)
        m_new = jnp.maximum(m, jnp.max(s, axis=-1, keepdims=True))
        alpha = jnp.exp(m - m_new)
        p = jnp.exp(s - m_new)
        l = alpha * l + jnp.sum(p, axis=-1, keepdims=True)
        acc = alpha * acc + _dot(p.astype(BF16), v)
        return m_new, l, acc

    init = (jnp.full((T, 1), NEG, F32), jnp.zeros((T, 1), F32), jnp.zeros((T, HEAD_DIM), F32))
    _, l, acc = lax.fori_loop(0, qb + 1, body, init)
    o_ref[...] = (acc / l).astype(o_ref.dtype)


def _forgetting_attention(qkv, c, ct):
    s = qkv.shape[0]
    T = FOX_TQ
    q0 = 3 * WIDTH_A // HEAD_DIM
    k0 = q0 + N_HEADS_B
    v0 = k0 + N_HEADS_B
    ck = ct.reshape(HEAD_DIM, 1, s)
    return pl.pallas_call(
        _fox_kernel,
        grid=(N_HEADS_B, s // T),
        in_specs=[
            pl.BlockSpec((T, HEAD_DIM), lambda h, i: (i, q0 + h)),
            pl.BlockSpec((s, HEAD_DIM), lambda h, i: (0, k0 + h)),
            pl.BlockSpec((s, HEAD_DIM), lambda h, i: (0, v0 + h)),
            pl.BlockSpec((T, HEAD_DIM), lambda h, i: (i, 0)),
            pl.BlockSpec((None, 1, s), lambda h, i: (h, 0, 0)),
        ],
        out_specs=pl.BlockSpec((T, HEAD_DIM), lambda h, i: (i, h)),
        out_shape=jax.ShapeDtypeStruct((s, WIDTH_B), BF16),
        compiler_params=_cparams(("parallel", "arbitrary")),
        name="fox_attention",
    )(qkv, qkv, qkv, c, ck)


def _up_kernel(xn_ref, ya_ref, yb_ref, wga_ref, wgb_ref, wua_ref, wub_ref, o_ref):
    xn = xn_ref[...]
    ga = jax.nn.sigmoid(_dot(xn, wga_ref[...]))
    ua = _dot(ya_ref[...], wua_ref[...])
    part = ga * ua
    gb = jax.nn.sigmoid(_dot(xn, wgb_ref[...]))
    ub = _dot(yb_ref[...], wub_ref[...])
    o_ref[...] = (part + gb * ub).astype(o_ref.dtype)


def _gated_up(xn, ya, yb, w_ga, w_gb, w_up_a, w_up_b):
    s, d = xn.shape
    row = lambda width: pl.BlockSpec((UP_TM, width), lambda i, j: (i, 0))
    col = lambda depth: pl.BlockSpec((depth, UP_TN), lambda i, j: (0, j))
    return pl.pallas_call(
        _up_kernel,
        grid=(s // UP_TM, d // UP_TN),
        in_specs=[row(d), row(WIDTH_A_OUT), row(WIDTH_B), col(d), col(d), col(WIDTH_A_OUT), col(WIDTH_B)],
        out_specs=pl.BlockSpec((UP_TM, UP_TN), lambda i, j: (i, j)),
        out_shape=jax.ShapeDtypeStruct((s, d), BF16),
        compiler_params=_cparams(("parallel", "arbitrary"), big=True),
        name="gated_up",
    )(xn, ya, yb, w_ga, w_gb, w_up_a, w_up_b)


def _out_kernel(a_ref, w_ref, x_ref, o_ref):
    o_ref[...] = x_ref[...] + _dot(a_ref[...], w_ref[...])


def _out_proj(merged, w_out, x2d):
    s, d = x2d.shape
    return pl.pallas_call(
        _out_kernel,
        grid=(s // OUT_TM, d // OUT_TN),
        in_specs=[
            pl.BlockSpec((OUT_TM, d), lambda i, j: (i, 0)),
            pl.BlockSpec((d, OUT_TN), lambda i, j: (0, j)),
            pl.BlockSpec((OUT_TM, OUT_TN), lambda i, j: (i, j)),
        ],
        out_specs=pl.BlockSpec((OUT_TM, OUT_TN), lambda i, j: (i, j)),
        out_shape=jax.ShapeDtypeStruct((s, d), F32),
        compiler_params=_cparams(("parallel", "arbitrary"), big=True),
        name="out_proj",
    )(merged, w_out, x2d)


def _peer_scores_kernel(wq_ref, hn_ref, sk_ref, o_ref):
    qt = _dot_nt(wq_ref[...], hn_ref[...]).astype(BF16)
    half = PEER_QDIM // 2
    for c in range(2):
        o_ref[c * N_KEYS:(c + 1) * N_KEYS, :] = _dot(sk_ref[c * N_KEYS:(c + 1) * N_KEYS, :], qt[c * half:(c + 1) * half, :])


def _peer_scores(hn, wq_t, sk):
    s, d = hn.shape
    rows = PEER_HEADS * 2 * N_KEYS
    return pl.pallas_call(
        _peer_scores_kernel,
        grid=(s // PQ_TM, PEER_HEADS),
        in_specs=[
            pl.BlockSpec((PEER_QDIM, d), lambda i, j: (j, 0)),
            pl.BlockSpec((PQ_TM, d), lambda i, j: (i, 0)),
            pl.BlockSpec((2 * N_KEYS, PEER_QDIM // 2), lambda i, j: (j, 0)),
        ],
        out_specs=pl.BlockSpec((2 * N_KEYS, PQ_TM), lambda i, j: (j, i)),
        out_shape=jax.ShapeDtypeStruct((rows, s), F32),
        compiler_params=_cparams(("parallel", "arbitrary")),
        name="peer_scores",
    )(wq_t, hn, sk)


def _stair_width(a):
    return PEER_TOPK // (a + 1)


def _topk_kernel(sc_ref, e1_ref, w2_ref, thr_ref):
    K = PEER_TOPK
    s1 = sc_ref[0:N_KEYS, :]
    s2 = sc_ref[N_KEYS:2 * N_KEYS, :]
    ninf = -jnp.inf

    def top(cur, count):
        outs = []
        for _ in range(count):
            mk = jnp.max(cur, axis=0, keepdims=True)
            outs.append(mk)
            cur = jnp.where(cur == mk, ninf, cur)
        return outs

    t1 = top(s1, K)
    t2 = top(s2, K)
    t2_all = jnp.concatenate(t2, axis=0)
    rank = lax.broadcasted_iota(jnp.int32, t2_all.shape, 0)
    n_wide = K // 2
    blocks = [jnp.where(rank < _stair_width(a), t1[a] + t2_all, ninf) for a in range(n_wide)]
    blocks.append(jnp.concatenate(t1[n_wide:], axis=0) + t2[0])
    cand = jnp.concatenate(blocks, axis=0)
    thr = top(cand, K)[K - 1]
    z = jnp.sum(jnp.where(cand >= thr, jnp.exp(cand - (t1[0] + t2[0])), 0.0), axis=0, keepdims=True)
    e1_ref[...] = jnp.exp(s1 - t1[0])
    w2_ref[...] = jnp.exp(s2 - t2[0]) / z
    thr_ref[...] = thr


def _peer_select(sc_t):
    s = sc_t.shape[1]
    rows = PEER_HEADS * N_KEYS
    return pl.pallas_call(
        _topk_kernel,
        grid=(s // TOPK_TL, PEER_HEADS),
        in_specs=[pl.BlockSpec((2 * N_KEYS, TOPK_TL), lambda i, j: (j, i))],
        out_specs=[
            pl.BlockSpec((N_KEYS, TOPK_TL), lambda i, j: (j, i)),
            pl.BlockSpec((N_KEYS, TOPK_TL), lambda i, j: (j, i)),
            pl.BlockSpec((None, 1, TOPK_TL), lambda i, j: (j, 0, i)),
        ],
        out_shape=[
            jax.ShapeDtypeStruct((rows, s), F32),
            jax.ShapeDtypeStruct((rows, s), F32),
            jax.ShapeDtypeStruct((PEER_HEADS, 1, s), F32),
        ],
        compiler_params=_cparams(("parallel", "arbitrary")),
        name="peer_select",
    )(sc_t)


def _peer_kernel(hn_ref, u_ref, vt_ref, sc_ref, s1_ref, e1_ref, w2_ref, thr_ref, o_ref, a_ref, ag_ref):
    ej = pl.program_id(1)

    @pl.when(ej == 0)
    def _():
        o_ref[...] = jnp.zeros_like(o_ref)

    a_ref[...] = _dot_nt(u_ref[...], hn_ref[...])
    inv_sqrt2 = 0.7071067811865476
    for b in range(PEER_TE // N_KEYS):
        i1 = ej * (PEER_TE // N_KEYS) + b
        rows = slice(b * N_KEYS, (b + 1) * N_KEYS)
        for lc in range(PEER_TM // PEER_LC):
            cols = slice(lc * PEER_LC, (lc + 1) * PEER_LC)
            g = jnp.zeros((N_KEYS, PEER_LC), F32)
            for h in range(PEER_HEADS):
                s1 = s1_ref[i1, h:h + 1, cols]
                e1 = e1_ref[i1, h:h + 1, cols]
                s2 = sc_ref[h * 2 * N_KEYS + N_KEYS:(h + 1) * 2 * N_KEYS, cols]
                w2 = w2_ref[h * N_KEYS:(h + 1) * N_KEYS, cols]
                g = g + jnp.where(s2 + s1 >= thr_ref[h, :, cols], w2 * e1, 0.0)
            a = a_ref[rows, cols]
            act = 0.5 * a * (1.0 + lax.erf(a * inv_sqrt2))
            ag_ref[rows, cols] = (act * g).astype(BF16)
    o_ref[...] += _dot(vt_ref[...], ag_ref[...])


def _peer_dense(hn, u, vt, sc_t, e1, w2, thr):
    s, d = hn.shape
    once = pl.Buffered(1)
    s1_r = sc_t.reshape(PEER_HEADS, 2, N_KEYS, s)[:, 0].transpose(1, 0, 2)
    e1_r = e1.reshape(PEER_HEADS, N_KEYS, s).transpose(1, 0, 2)
    by_key = pl.BlockSpec((N_KEYS, PEER_HEADS, PEER_TM), lambda i, j: (0, 0, i), pipeline_mode=once)
    return pl.pallas_call(
        _peer_kernel,
        grid=(s // PEER_TM, N_EXPERTS // PEER_TE),
        in_specs=[
            pl.BlockSpec((PEER_TM, d), lambda i, j: (i, 0), pipeline_mode=once),
            pl.BlockSpec((PEER_TE, d), lambda i, j: (j, 0)),
            pl.BlockSpec((d, PEER_TE), lambda i, j: (0, j)),
            pl.BlockSpec((PEER_HEADS * 2 * N_KEYS, PEER_TM), lambda i, j: (0, i), pipeline_mode=once),
            by_key,
            by_key,
            pl.BlockSpec((PEER_HEADS * N_KEYS, PEER_TM), lambda i, j: (0, i), pipeline_mode=once),
            pl.BlockSpec((PEER_HEADS, 1, PEER_TM), lambda i, j: (0, 0, i), pipeline_mode=once),
        ],
        out_specs=pl.BlockSpec((d, PEER_TM), lambda i, j: (0, i), pipeline_mode=once),
        out_shape=jax.ShapeDtypeStruct((d, s), F32),
        scratch_shapes=[pltpu.VMEM((PEER_TE, PEER_TM), F32), pltpu.VMEM((PEER_TE, PEER_TM), BF16)],
        compiler_params=_cparams(("parallel", "arbitrary"), big=True),
        name="peer_dense",
    )(hn, u, vt, sc_t, s1_r, e1_r, w2, thr)


def _final_kernel(h_ref, pt_ref, o_ref):
    o_ref[...] = h_ref[...] + pt_ref[...].T


def _final_add(h, peer_t):
    s, d = h.shape
    return pl.pallas_call(
        _final_kernel,
        grid=(s // FINAL_TM,),
        in_specs=[pl.BlockSpec((FINAL_TM, d), lambda i: (i, 0)), pl.BlockSpec((d, FINAL_TM), lambda i: (0, i))],
        out_specs=pl.BlockSpec((FINAL_TM, d), lambda i: (i, 0)),
        out_shape=jax.ShapeDtypeStruct((s, d), F32),
        compiler_params=_cparams(("parallel",)),
        name="final_add",
    )(h, peer_t)


def _layer(h, norm1_gain, w_in, b_forget, q_norm_a, k_norm_a, q_norm_b, k_norm_b,
           w_up_a, w_up_b, w_out, norm2_gain, w_peer_q, peer_subkeys, peer_u, peer_v):
    d = D_MODEL
    scale = HEAD_DIM ** -0.5
    f_lo = QKV_COLS
    g_lo = f_lo + N_HEADS_B
    w_qkv = w_in[:, :f_lo].astype(BF16)
    w_f = jnp.pad(w_in[:, f_lo:g_lo], ((0, 0), (0, HEAD_DIM - N_HEADS_B))).astype(BF16)
    b_f = jnp.pad(b_forget.astype(F32), (0, HEAD_DIM - N_HEADS_B)).reshape(1, HEAD_DIM)
    w_ga = w_in[:, g_lo:g_lo + d].astype(BF16)
    w_gb = w_in[:, g_lo + d:g_lo + 2 * d].astype(BF16)
    ones = jnp.ones((HEAD_DIM,), F32)
    gain = jnp.concatenate([
        jnp.tile(q_norm_a.astype(F32) * scale, N_HEADS_A), jnp.tile(k_norm_a.astype(F32), N_HEADS_A), jnp.tile(ones, N_HEADS_A),
        jnp.tile(q_norm_b.astype(F32) * scale, N_HEADS_B), jnp.tile(k_norm_b.astype(F32), N_HEADS_B), jnp.tile(ones, N_HEADS_B),
    ]).reshape(1, QKV_COLS)
    flag = jnp.concatenate([
        jnp.ones((2 * WIDTH_A,), F32), jnp.zeros((WIDTH_A,), F32), jnp.ones((2 * WIDTH_B,), F32), jnp.zeros((WIDTH_B,), F32),
    ]).reshape(1, QKV_COLS)

    xn = _rmsnorm_bf16(h, norm1_gain)
    qkv = _qkv_proj(xn, w_qkv, gain, flag)
    c, ct = _forget_cumsum(xn, w_f, b_f)
    y_a = _dilated_mixture(qkv)
    y_b = _forgetting_attention(qkv, c, ct)
    merged = _gated_up(xn, y_a, y_b, w_ga, w_gb, w_up_a.astype(BF16), w_up_b.astype(BF16))
    h = _out_proj(merged, w_out.astype(BF16), h)

    hn = _rmsnorm_bf16(h, norm2_gain)
    wq_t = w_peer_q.T.astype(BF16)
    sk = peer_subkeys.reshape(PEER_HEADS * 2 * N_KEYS, PEER_QDIM // 2).astype(BF16)
    sc_t = _peer_scores(hn, wq_t, sk)
    e1, w2, thr = _peer_select(sc_t)
    peer_t = _peer_dense(hn, peer_u.astype(BF16), peer_v.T.astype(BF16), sc_t, e1, w2, thr)
    return _final_add(h, peer_t)


def kernel(x, norm1_gain, w_in, b_forget, q_norm_a, k_norm_a, q_norm_b, k_norm_b,
           w_up_a, w_up_b, w_out, norm2_gain, w_peer_q, peer_subkeys, peer_u, peer_v):
    b, s, d = x.shape
    assert b == 1 and d == D_MODEL and s % (DILATION_GROUPS[-1][0]) == 0
    h = x.reshape(s, d)
    for layer in range(norm1_gain.shape[0]):
        h = _layer(h, norm1_gain[layer], w_in[layer], b_forget[layer], q_norm_a[layer], k_norm_a[layer],
                   q_norm_b[layer], k_norm_b[layer], w_up_a[layer], w_up_b[layer], w_out[layer], norm2_gain[layer],
                   w_peer_q[layer], peer_subkeys[layer], peer_u[layer], peer_v[layer])
    return h.reshape(b, s, d)
```

```python
import functools

import numpy as np
import jax
import jax.numpy as jnp
from jax import lax
from jax.experimental import pallas as pl
from jax.experimental.pallas import tpu as pltpu

F32 = jnp.float32
BF16 = jnp.bfloat16

D_MODEL = 4096
HEAD_DIM = 128
DILATION_GROUPS = ((128, 1), (512, 4), (2048, 16))
A_SLOTS = 6
N_HEADS_A = A_SLOTS * len(DILATION_GROUPS)
N_HEADS_B = D_MODEL // HEAD_DIM - N_HEADS_A
WIDTH_A = N_HEADS_A * HEAD_DIM
WIDTH_B = N_HEADS_B * HEAD_DIM
WIDTH_A_OUT = A_SLOTS * HEAD_DIM
QKV_COLS = 3 * WIDTH_A + 3 * WIDTH_B
ALIBI_MAX_EXP = 8.0
PEER_HEADS = 8
N_KEYS = 128
N_EXPERTS = N_KEYS * N_KEYS
PEER_TOPK = 16
PEER_QDIM = 256
EPS = 1e-6
NEG = -1e30
LOG2E = 1.4426950408889634
DIL_BLOCK = 128

VMEM_LIMIT_BYTES = 52 * 1024 * 1024

NORM_TM = 512
QKV_TM, QKV_TN = 1024, 512
FORGET_TM = 256
FOX_TQ = 512
FOX_ROWS = 16
MERGE_TM = 512
UP_TM, UP_TN = 512, 512
OUT_TM, OUT_TN = 1024, 512
PQ_TM = 512
TOPK_TL = 256
PEER_TM, PEER_TE = 512, 512
PEER_LC = 128
FINAL_TM = 256


def _cparams(sem, big=False):
    return pltpu.CompilerParams(dimension_semantics=sem, vmem_limit_bytes=VMEM_LIMIT_BYTES if big else None)


def _dot(a, b):
    return jnp.dot(a, b, preferred_element_type=F32)


def _dot_nt(a, b):
    return lax.dot_general(a, b, (((1,), (1,)), ((), ())), preferred_element_type=F32)


def _rmsnorm_kernel(x_ref, g_ref, o_ref):
    x = x_ref[...]
    ms = jnp.mean(x * x, axis=-1, keepdims=True)
    o_ref[...] = (x * lax.rsqrt(ms + EPS) * g_ref[...]).astype(o_ref.dtype)


def _rmsnorm_bf16(x2d, gain):
    s, d = x2d.shape
    return pl.pallas_call(
        _rmsnorm_kernel,
        grid=(s // NORM_TM,),
        in_specs=[pl.BlockSpec((NORM_TM, d), lambda i: (i, 0)), pl.BlockSpec((1, d), lambda i: (0, 0))],
        out_specs=pl.BlockSpec((NORM_TM, d), lambda i: (i, 0)),
        out_shape=jax.ShapeDtypeStruct((s, d), BF16),
        compiler_params=_cparams(("parallel",)),
        name="rmsnorm",
    )(x2d, gain.reshape(1, d))


def _qkv_kernel(a_ref, w_ref, gain_ref, flag_ref, o_ref):
    acc = _dot(a_ref[...], w_ref[...])
    for c in range(QKV_TN // HEAD_DIM):
        sl = slice(c * HEAD_DIM, (c + 1) * HEAD_DIM)
        y = acc[:, sl]
        ms = jnp.mean(y * y, axis=-1, keepdims=True)
        yn = y * lax.rsqrt(ms + EPS) * gain_ref[:, sl]
        o_ref[:, sl] = jnp.where(flag_ref[:, sl] > 0.0, yn, y).astype(o_ref.dtype)


def _qkv_proj(xn, w_qkv, gain, flag):
    s, d = xn.shape
    n = w_qkv.shape[1]
    return pl.pallas_call(
        _qkv_kernel,
        grid=(s // QKV_TM, n // QKV_TN),
        in_specs=[
            pl.BlockSpec((QKV_TM, d), lambda i, j: (i, 0)),
            pl.BlockSpec((d, QKV_TN), lambda i, j: (0, j)),
            pl.BlockSpec((1, QKV_TN), lambda i, j: (0, j)),
            pl.BlockSpec((1, QKV_TN), lambda i, j: (0, j)),
        ],
        out_specs=pl.BlockSpec((QKV_TM, QKV_TN), lambda i, j: (i, j)),
        out_shape=jax.ShapeDtypeStruct((s, n), BF16),
        compiler_params=_cparams(("parallel", "arbitrary"), big=True),
        name="qkv_proj",
    )(xn, w_qkv, gain, flag)


def _forget_kernel(a_ref, w_ref, b_ref, c_ref, ct_ref, carry_ref):
    @pl.when(pl.program_id(0) == 0)
    def _():
        carry_ref[...] = jnp.zeros_like(carry_ref)

    f = _dot(a_ref[...], w_ref[...]) + b_ref[...]
    logf = jnp.minimum(f, 0.0) - jnp.log1p(jnp.exp(-jnp.abs(f)))
    r = lax.broadcasted_iota(jnp.int32, (FORGET_TM, FORGET_TM), 0)
    c = lax.broadcasted_iota(jnp.int32, (FORGET_TM, FORGET_TM), 1)
    tri = (c <= r).astype(F32)
    cs = jnp.dot(tri, logf, precision=lax.Precision.HIGHEST, preferred_element_type=F32) + carry_ref[...]
    cs2 = cs * LOG2E
    c_ref[...] = cs2
    ct_ref[...] = cs2.T
    carry_ref[...] = cs[FORGET_TM - 1:FORGET_TM, :]


def _forget_cumsum(xn, w_f, b_f):
    s, d = xn.shape
    return pl.pallas_call(
        _forget_kernel,
        grid=(s // FORGET_TM,),
        in_specs=[
            pl.BlockSpec((FORGET_TM, d), lambda i: (i, 0)),
            pl.BlockSpec((d, HEAD_DIM), lambda i: (0, 0)),
            pl.BlockSpec((1, HEAD_DIM), lambda i: (0, 0)),
        ],
        out_specs=[pl.BlockSpec((FORGET_TM, HEAD_DIM), lambda i: (i, 0)), pl.BlockSpec((HEAD_DIM, FORGET_TM), lambda i: (0, i))],
        out_shape=[jax.ShapeDtypeStruct((s, HEAD_DIM), F32), jax.ShapeDtypeStruct((HEAD_DIM, s), F32)],
        scratch_shapes=[pltpu.VMEM((1, HEAD_DIM), F32)],
        compiler_params=_cparams(("arbitrary",)),
        name="forget_cumsum",
    )(xn, w_f, b_f)


def _alibi_slopes():
    n = N_HEADS_A
    return np.exp2(-np.float32(ALIBI_MAX_EXP) * np.arange(1, n + 1, dtype=np.float32) / np.float32(n)).astype(np.float32)


def _dilated_kernel(q_ref, kp_ref, kc_ref, vp_ref, vc_ref, o_ref, m_ref, l_ref, *, step_slopes):
    n = pl.program_id(1)
    L = DIL_BLOCK
    qi = lax.broadcasted_iota(jnp.int32, (L, L), 0)
    kj = lax.broadcasted_iota(jnp.int32, (L, L), 1)
    d_cur = qi - kj
    d_prev = d_cur + L
    valid_cur = d_cur >= 0
    valid_prev = d_prev <= jnp.where(n > 0, L, -1)
    d_cur_f = d_cur.astype(F32)
    d_prev_f = d_prev.astype(F32)
    for j in range(A_SLOTS):
        sl = slice(j * HEAD_DIM, (j + 1) * HEAD_DIM)
        q = q_ref[:, sl]
        slope = float(step_slopes[j])
        s_c = jnp.where(valid_cur, _dot_nt(q, kc_ref[:, sl]) - slope * d_cur_f, NEG)
        s_p = jnp.where(valid_prev, _dot_nt(q, kp_ref[:, sl]) - slope * d_prev_f, NEG)
        m = jnp.maximum(jnp.max(s_c, axis=-1, keepdims=True), jnp.max(s_p, axis=-1, keepdims=True))
        p_c = jnp.exp(s_c - m)
        p_p = jnp.exp(s_p - m)
        l = jnp.sum(p_c, axis=-1, keepdims=True) + jnp.sum(p_p, axis=-1, keepdims=True)
        o_ref[:, sl] = _dot(p_c.astype(BF16), vc_ref[:, sl]) + _dot(p_p.astype(BF16), vp_ref[:, sl])
        m_ref[:, sl] = jnp.broadcast_to(m, (L, HEAD_DIM))
        l_ref[:, sl] = jnp.broadcast_to(l, (L, HEAD_DIM))


def _dilated_group(qkv, g, dilation):
    s = qkv.shape[0]
    r = dilation
    L = DIL_BLOCK
    view = qkv.reshape(s // r, r * QKV_COLS)
    per_row = QKV_COLS // WIDTH_A_OUT
    k_off = WIDTH_A // WIDTH_A_OUT
    v_off = 2 * WIDTH_A // WIDTH_A_OUT
    slopes = _alibi_slopes()[g * A_SLOTS:(g + 1) * A_SLOTS] * np.float32(r)

    def cur(off):
        return pl.BlockSpec((L, WIDTH_A_OUT), lambda c, n: (n, c * per_row + off + g))

    def prev(off):
        return pl.BlockSpec((L, WIDTH_A_OUT), lambda c, n: (jnp.maximum(n - 1, 0), c * per_row + off + g))

    out_spec = pl.BlockSpec((L, WIDTH_A_OUT), lambda c, n: (n, c))
    out_sds = jax.ShapeDtypeStruct((s // r, r * WIDTH_A_OUT), F32)
    o, m, l = pl.pallas_call(
        functools.partial(_dilated_kernel, step_slopes=tuple(float(v) for v in slopes)),
        grid=(r, s // (r * L)),
        in_specs=[cur(0), prev(k_off), cur(k_off), prev(v_off), cur(v_off)],
        out_specs=[out_spec, out_spec, out_spec],
        out_shape=[out_sds, out_sds, out_sds],
        compiler_params=_cparams(("parallel", "arbitrary")),
        name=f"dilated_g{g}",
    )(view, view, view, view, view)
    shape = (s, WIDTH_A_OUT)
    return o.reshape(shape), m.reshape(shape), l.reshape(shape)


def _merge_kernel(*refs):
    n = len(DILATION_GROUPS)
    o_refs, m_refs, l_refs, y_ref = refs[:n], refs[n:2 * n], refs[2 * n:3 * n], refs[3 * n]
    ms = [r[...] for r in m_refs]
    m_star = functools.reduce(jnp.maximum, ms)
    num = 0.0
    den = 0.0
    for o_r, m, l_r in zip(o_refs, ms, l_refs):
        w = jnp.exp(m - m_star)
        num = num + w * o_r[...]
        den = den + w * l_r[...]
    y_ref[...] = (num / den).astype(y_ref.dtype)


def _dilated_mixture(qkv):
    s = qkv.shape[0]
    res = [_dilated_group(qkv, g, dil) for g, (_, dil) in enumerate(DILATION_GROUPS)]
    spec = pl.BlockSpec((MERGE_TM, WIDTH_A_OUT), lambda i: (i, 0))
    args = [r[0] for r in res] + [r[1] for r in res] + [r[2] for r in res]
    return pl.pallas_call(
        _merge_kernel,
        grid=(s // MERGE_TM,),
        in_specs=[spec] * len(args),
        out_specs=spec,
        out_shape=jax.ShapeDtypeStruct((s, WIDTH_A_OUT), BF16),
        compiler_params=_cparams(("parallel",)),
        name="dilated_merge",
    )(*args)


def _fox_kernel(q_ref, k_ref, v_ref, cq_ref, ck_ref, o_ref, s0_ref, s1_ref, p0_ref, p1_ref, cqr_ref, m_ref, l_ref,
                alpha_ref, acc_ref):
    h = pl.program_id(0)
    qb = pl.program_id(1)
    T = FOX_TQ
    R = FOX_ROWS
    G = T // HEAD_DIM
    s_refs = (s0_ref, s1_ref)
    p_refs = (p0_ref, p1_ref)
    lane = lax.broadcasted_iota(jnp.int32, (T, HEAD_DIM), 1)
    cq = jnp.sum(jnp.where(lane == h, cq_ref[...], 0.0), axis=-1, keepdims=True)
    cqr_ref[...] = jnp.broadcast_to(cq, (T, HEAD_DIM))
    m_ref[...] = jnp.full((T, HEAD_DIM), NEG, F32)
    l_ref[...] = jnp.zeros((T, HEAD_DIM), F32)
    acc_ref[...] = jnp.zeros((T, HEAD_DIM), F32)
    p1_ref[...] = jnp.zeros((T, T), BF16)

    def keys(j):
        return pl.ds(pl.multiple_of(j * T, T), T)

    def logits(j, slot):
        s_refs[slot][...] = _dot_nt(q_ref[...], k_ref[keys(j), :])

    def softmax(j, slot, diagonal):
        s_ref, p_ref = s_refs[slot], p_refs[slot]
        start = pl.multiple_of(j * T, T)
        for rc in range(T // R):
            rows = slice(rc * R, (rc + 1) * R)
            cq_r = cqr_ref[rows, :]
            sg = []
            for g in range(G):
                lo = g * HEAD_DIM
                if diagonal and lo > rc * R + R - 1:
                    sg.append(None)
                    continue
                t = s_ref[rows, lo:lo + HEAD_DIM] + cq_r - ck_ref[:, pl.ds(start + lo, HEAD_DIM)]
                if diagonal and lo + HEAD_DIM - 1 > rc * R:
                    rel = (lax.broadcasted_iota(jnp.int32, (R, HEAD_DIM), 0) - lax.broadcasted_iota(jnp.int32, (R, HEAD_DIM), 1))
                    t = jnp.where(rel >= lo - rc * R, t, NEG)
                sg.append(t)
            live = [t for t in sg if t is not None]
            m_old = m_ref[rows, :]
            m_new = jnp.maximum(m_old, jnp.max(functools.reduce(jnp.maximum, live), axis=-1, keepdims=True))
            alpha = jnp.exp2(m_old - m_new)
            ps = [None if t is None else jnp.exp2(t - m_new) for t in sg]
            row_sum = jnp.sum(functools.reduce(jnp.add, [p for p in ps if p is not None]), axis=-1, keepdims=True)
            l_ref[rows, :] = alpha * l_ref[rows, :] + row_sum
            m_ref[rows, :] = m_new
            alpha_ref[rows, :] = alpha
            for g in range(G):
                lo = g * HEAD_DIM
                p_ref[rows, lo:lo + HEAD_DIM] = (jnp.zeros((R, HEAD_DIM), BF16) if ps[g] is None else ps[g].astype(BF16))

    def stage(j, slot, diagonal):
        other = 1 - slot
        if not diagonal:
            logits(j + 1, other)
        pv = _dot(p_refs[other][...], v_ref[keys(jnp.maximum(j - 1, 0)), :])
        softmax(j, slot, diagonal)
        acc_ref[...] = (acc_ref[...] + pv) * alpha_ref[...]

    logits(0, 0)

    def pair(i, carry):
        stage(2 * i, 0, False)
        stage(2 * i + 1, 1, False)
        return carry

    lax.fori_loop(0, qb // 2, pair, 0)

    def finish(slot):
        stage(qb, slot, True)
        acc = acc_ref[...] + _dot(p_refs[slot][...], v_ref[keys(qb), :])
        o_ref[...] = (acc / l_ref[...]).astype(o_ref.dtype)

    @pl.when(qb % 2 == 0)
    def _():
        finish(0)

    @pl.when(qb % 2 == 1)
    def _():
        stage(qb - 1, 0, False)
        finish(1)


def _forgetting_attention(qkv, c, ct):
    s = qkv.shape[0]
    T = FOX_TQ
    q0 = 3 * WIDTH_A // HEAD_DIM
    k0 = q0 + N_HEADS_B
    v0 = k0 + N_HEADS_B
    ck = ct.reshape(HEAD_DIM, 1, s)
    return pl.pallas_call(
        _fox_kernel,
        grid=(N_HEADS_B, s // T),
        in_specs=[
            pl.BlockSpec((T, HEAD_DIM), lambda h, i: (i, q0 + h)),
            pl.BlockSpec((s, HEAD_DIM), lambda h, i: (0, k0 + h)),
            pl.BlockSpec((s, HEAD_DIM), lambda h, i: (0, v0 + h)),
            pl.BlockSpec((T, HEAD_DIM), lambda h, i: (i, 0)),
            pl.BlockSpec((None, 1, s), lambda h, i: (h, 0, 0)),
        ],
        out_specs=pl.BlockSpec((T, HEAD_DIM), lambda h, i: (i, h)),
        out_shape=jax.ShapeDtypeStruct((s, WIDTH_B), BF16),
        scratch_shapes=[
            pltpu.VMEM((T, T), F32), pltpu.VMEM((T, T), F32), pltpu.VMEM((T, T), BF16), pltpu.VMEM((T, T), BF16),
            pltpu.VMEM((T, HEAD_DIM), F32), pltpu.VMEM((T, HEAD_DIM), F32), pltpu.VMEM((T, HEAD_DIM), F32),
            pltpu.VMEM((T, HEAD_DIM), F32), pltpu.VMEM((T, HEAD_DIM), F32),
        ],
        compiler_params=_cparams(("parallel", "arbitrary")),
        name="fox_attention",
    )(qkv, qkv, qkv, c, ck)


def _up_kernel(xn_ref, ya_ref, yb_ref, wga_ref, wgb_ref, wua_ref, wub_ref, o_ref):
    xn = xn_ref[...]
    ga = jax.nn.sigmoid(_dot(xn, wga_ref[...]))
    ua = _dot(ya_ref[...], wua_ref[...])
    part = ga * ua
    gb = jax.nn.sigmoid(_dot(xn, wgb_ref[...]))
    ub = _dot(yb_ref[...], wub_ref[...])
    o_ref[...] = (part + gb * ub).astype(o_ref.dtype)


def _gated_up(xn, ya, yb, w_ga, w_gb, w_up_a, w_up_b):
    s, d = xn.shape
    row = lambda width: pl.BlockSpec((UP_TM, width), lambda i, j: (i, 0))
    col = lambda depth: pl.BlockSpec((depth, UP_TN), lambda i, j: (0, j))
    return pl.pallas_call(
        _up_kernel,
        grid=(s // UP_TM, d // UP_TN),
        in_specs=[row(d), row(WIDTH_A_OUT), row(WIDTH_B), col(d), col(d), col(WIDTH_A_OUT), col(WIDTH_B)],
        out_specs=pl.BlockSpec((UP_TM, UP_TN), lambda i, j: (i, j)),
        out_shape=jax.ShapeDtypeStruct((s, d), BF16),
        compiler_params=_cparams(("parallel", "arbitrary"), big=True),
        name="gated_up",
    )(xn, ya, yb, w_ga, w_gb, w_up_a, w_up_b)


def _out_kernel(a_ref, w_ref, x_ref, o_ref):
    o_ref[...] = x_ref[...] + _dot(a_ref[...], w_ref[...])


def _out_proj(merged, w_out, x2d):
    s, d = x2d.shape
    return pl.pallas_call(
        _out_kernel,
        grid=(s // OUT_TM, d // OUT_TN),
        in_specs=[
            pl.BlockSpec((OUT_TM, d), lambda i, j: (i, 0)),
            pl.BlockSpec((d, OUT_TN), lambda i, j: (0, j)),
            pl.BlockSpec((OUT_TM, OUT_TN), lambda i, j: (i, j)),
        ],
        out_specs=pl.BlockSpec((OUT_TM, OUT_TN), lambda i, j: (i, j)),
        out_shape=jax.ShapeDtypeStruct((s, d), F32),
        compiler_params=_cparams(("parallel", "arbitrary"), big=True),
        name="out_proj",
    )(merged, w_out, x2d)


def _peer_scores_kernel(wq_ref, hn_ref, sk_ref, o_ref):
    qt = _dot_nt(wq_ref[...], hn_ref[...]).astype(BF16)
    half = PEER_QDIM // 2
    for c in range(2):
        o_ref[c * N_KEYS:(c + 1) * N_KEYS, :] = _dot(sk_ref[c * N_KEYS:(c + 1) * N_KEYS, :], qt[c * half:(c + 1) * half, :])


def _peer_scores(hn, wq_t, sk):
    s, d = hn.shape
    rows = PEER_HEADS * 2 * N_KEYS
    return pl.pallas_call(
        _peer_scores_kernel,
        grid=(s // PQ_TM, PEER_HEADS),
        in_specs=[
            pl.BlockSpec((PEER_QDIM, d), lambda i, j: (j, 0)),
            pl.BlockSpec((PQ_TM, d), lambda i, j: (i, 0)),
            pl.BlockSpec((2 * N_KEYS, PEER_QDIM // 2), lambda i, j: (j, 0)),
        ],
        out_specs=pl.BlockSpec((2 * N_KEYS, PQ_TM), lambda i, j: (j, i)),
        out_shape=jax.ShapeDtypeStruct((rows, s), F32),
        compiler_params=_cparams(("parallel", "arbitrary")),
        name="peer_scores",
    )(wq_t, hn, sk)


def _stair_width(a):
    return PEER_TOPK // (a + 1)


def _topk_kernel(sc_ref, e1_ref, w2_ref, tau_ref):
    K = PEER_TOPK
    s1 = sc_ref[0:N_KEYS, :]
    s2 = sc_ref[N_KEYS:2 * N_KEYS, :]
    ninf = -jnp.inf

    def top(cur, count):
        outs = []
        for _ in range(count):
            mk = jnp.max(cur, axis=0, keepdims=True)
            outs.append(mk)
            cur = jnp.where(cur == mk, ninf, cur)
        return outs

    t1 = top(s1, K)
    t2 = top(s2, K)
    t2_all = jnp.concatenate(t2, axis=0)
    rank = lax.broadcasted_iota(jnp.int32, t2_all.shape, 0)
    n_wide = K // 2
    blocks = [jnp.where(rank < _stair_width(a), t1[a] + t2_all, ninf) for a in range(n_wide)]
    blocks.append(jnp.concatenate(t1[n_wide:], axis=0) + t2[0])
    cand = jnp.concatenate(blocks, axis=0)
    thr = top(cand, K)[K - 1]
    z = jnp.sum(jnp.where(cand >= thr, jnp.exp(cand - (t1[0] + t2[0])), 0.0), axis=0, keepdims=True)
    e1_ref[...] = jnp.exp(s1 - t1[0])
    w2_ref[...] = jnp.exp(s2 - t2[0]) / z
    tau = jnp.full(s1.shape, jnp.inf, F32)
    for b in range(K):
        tau = jnp.where(s1 + t2[b] >= thr, t2[b], tau)
    tau_ref[...] = tau


def _peer_select(sc_t):
    s = sc_t.shape[1]
    rows = PEER_HEADS * N_KEYS
    spec = pl.BlockSpec((N_KEYS, TOPK_TL), lambda i, j: (j, i))
    sds = jax.ShapeDtypeStruct((rows, s), F32)
    return pl.pallas_call(
        _topk_kernel,
        grid=(s // TOPK_TL, PEER_HEADS),
        in_specs=[pl.BlockSpec((2 * N_KEYS, TOPK_TL), lambda i, j: (j, i))],
        out_specs=[spec, spec, spec],
        out_shape=[sds, sds, sds],
        compiler_params=_cparams(("parallel", "arbitrary")),
        name="peer_select",
    )(sc_t)


def _peer_kernel(hn_ref, u_ref, vt_ref, s2_ref, tau_ref, e1_ref, w2_ref, o_ref, a0_ref, a1_ref, ag0_ref, ag1_ref):
    ej = pl.program_id(1)
    HALF = PEER_TE // 2
    KEY_BLOCKS = HALF // N_KEYS

    @pl.when(ej == 0)
    def _():
        o_ref[...] = jnp.zeros_like(o_ref)

    inv_sqrt2 = 0.7071067811865476

    def gate(a_ref, ag_ref, half):
        for b in range(KEY_BLOCKS):
            i1 = ej * (PEER_TE // N_KEYS) + half * KEY_BLOCKS + b
            rows = slice(b * N_KEYS, (b + 1) * N_KEYS)
            for lc in range(PEER_TM // PEER_LC):
                cols = slice(lc * PEER_LC, (lc + 1) * PEER_LC)
                g = jnp.zeros((N_KEYS, PEER_LC), F32)
                for h in range(PEER_HEADS):
                    tau = tau_ref[i1, h:h + 1, cols]
                    e1 = e1_ref[i1, h:h + 1, cols]
                    g = g + jnp.where(s2_ref[h, :, cols] >= tau, w2_ref[h * N_KEYS:(h + 1) * N_KEYS, cols], 0.0) * e1
                a = a_ref[rows, cols]
                act = 0.5 * a * (1.0 + lax.erf(a * inv_sqrt2))
                ag_ref[rows, cols] = (act * g).astype(BF16)

    a0_ref[...] = _dot_nt(u_ref[0:HALF, :], hn_ref[...])
    a1_ref[...] = _dot_nt(u_ref[HALF:PEER_TE, :], hn_ref[...])
    gate(a0_ref, ag0_ref, 0)
    gate(a1_ref, ag1_ref, 1)
    o_ref[...] += _dot(vt_ref[:, 0:HALF], ag0_ref[...]) + _dot(vt_ref[:, HALF:PEER_TE], ag1_ref[...])


def _peer_dense(hn, u, vt, sc_t, e1, w2, tau):
    s, d = hn.shape
    once = pl.Buffered(1)
    tau_r = tau.reshape(PEER_HEADS, N_KEYS, s).transpose(1, 0, 2)
    e1_r = e1.reshape(PEER_HEADS, N_KEYS, s).transpose(1, 0, 2)
    by_key = pl.BlockSpec((N_KEYS, PEER_HEADS, PEER_TM), lambda i, j: (0, 0, i), pipeline_mode=once)
    sc4 = sc_t.reshape(PEER_HEADS, 2, N_KEYS, s)
    half = PEER_TE // 2
    return pl.pallas_call(
        _peer_kernel,
        grid=(s // PEER_TM, N_EXPERTS // PEER_TE),
        in_specs=[
            pl.BlockSpec((PEER_TM, d), lambda i, j: (i, 0), pipeline_mode=once),
            pl.BlockSpec((PEER_TE, d), lambda i, j: (j, 0)),
            pl.BlockSpec((d, PEER_TE), lambda i, j: (0, j)),
            pl.BlockSpec((PEER_HEADS, None, N_KEYS, PEER_TM), lambda i, j: (0, 1, 0, i), pipeline_mode=once),
            by_key,
            by_key,
            pl.BlockSpec((PEER_HEADS * N_KEYS, PEER_TM), lambda i, j: (0, i), pipeline_mode=once),
        ],
        out_specs=pl.BlockSpec((d, PEER_TM), lambda i, j: (0, i), pipeline_mode=once),
        out_shape=jax.ShapeDtypeStruct((d, s), F32),
        scratch_shapes=[pltpu.VMEM((half, PEER_TM), F32), pltpu.VMEM((half, PEER_TM), F32),
                        pltpu.VMEM((half, PEER_TM), BF16), pltpu.VMEM((half, PEER_TM), BF16)],
        compiler_params=_cparams(("parallel", "arbitrary"), big=True),
        name="peer_dense",
    )(hn, u, vt, sc4, tau_r, e1_r, w2)


def _final_kernel(h_ref, pt_ref, o_ref):
    o_ref[...] = h_ref[...] + pt_ref[...].T


def _final_add(h, peer_t):
    s, d = h.shape
    return pl.pallas_call(
        _final_kernel,
        grid=(s // FINAL_TM,),
        in_specs=[pl.BlockSpec((FINAL_TM, d), lambda i: (i, 0)), pl.BlockSpec((d, FINAL_TM), lambda i: (0, i))],
        out_specs=pl.BlockSpec((FINAL_TM, d), lambda i: (i, 0)),
        out_shape=jax.ShapeDtypeStruct((s, d), F32),
        compiler_params=_cparams(("parallel",)),
        name="final_add",
    )(h, peer_t)


def _layer(h, norm1_gain, w_in, b_forget, q_norm_a, k_norm_a, q_norm_b, k_norm_b,
           w_up_a, w_up_b, w_out, norm2_gain, w_peer_q, peer_subkeys, peer_u, peer_v):
    d = D_MODEL
    scale = HEAD_DIM ** -0.5
    f_lo = QKV_COLS
    g_lo = f_lo + N_HEADS_B
    w_qkv = w_in[:, :f_lo].astype(BF16)
    w_f = jnp.pad(w_in[:, f_lo:g_lo], ((0, 0), (0, HEAD_DIM - N_HEADS_B))).astype(BF16)
    b_f = jnp.pad(b_forget.astype(F32), (0, HEAD_DIM - N_HEADS_B)).reshape(1, HEAD_DIM)
    w_ga = w_in[:, g_lo:g_lo + d].astype(BF16)
    w_gb = w_in[:, g_lo + d:g_lo + 2 * d].astype(BF16)
    ones = jnp.ones((HEAD_DIM,), F32)
    gain = jnp.concatenate([
        jnp.tile(q_norm_a.astype(F32) * scale, N_HEADS_A), jnp.tile(k_norm_a.astype(F32), N_HEADS_A), jnp.tile(ones, N_HEADS_A),
        jnp.tile(q_norm_b.astype(F32) * (scale * LOG2E), N_HEADS_B), jnp.tile(k_norm_b.astype(F32), N_HEADS_B), jnp.tile(ones, N_HEADS_B),
    ]).reshape(1, QKV_COLS)
    flag = jnp.concatenate([
        jnp.ones((2 * WIDTH_A,), F32), jnp.zeros((WIDTH_A,), F32), jnp.ones((2 * WIDTH_B,), F32), jnp.zeros((WIDTH_B,), F32),
    ]).reshape(1, QKV_COLS)

    xn = _rmsnorm_bf16(h, norm1_gain)
    qkv = _qkv_proj(xn, w_qkv, gain, flag)
    c, ct = _forget_cumsum(xn, w_f, b_f)
    y_a = _dilated_mixture(qkv)
    y_b = _forgetting_attention(qkv, c, ct)
    merged = _gated_up(xn, y_a, y_b, w_ga, w_gb, w_up_a.astype(BF16), w_up_b.astype(BF16))
    h = _out_proj(merged, w_out.astype(BF16), h)

    hn = _rmsnorm_bf16(h, norm2_gain)
    wq_t = w_peer_q.T.astype(BF16)
    sk = peer_subkeys.reshape(PEER_HEADS * 2 * N_KEYS, PEER_QDIM // 2).astype(BF16)
    sc_t = _peer_scores(hn, wq_t, sk)
    e1, w2, tau = _peer_select(sc_t)
    peer_t = _peer_dense(hn, peer_u.astype(BF16), peer_v.T.astype(BF16), sc_t, e1, w2, tau)
    return _final_add(h, peer_t)


def kernel(x, norm1_gain, w_in, b_forget, q_norm_a, k_norm_a, q_norm_b, k_norm_b,
           w_up_a, w_up_b, w_out, norm2_gain, w_peer_q, peer_subkeys, peer_u, peer_v):
    b, s, d = x.shape
    assert b == 1 and d == D_MODEL and s % (DILATION_GROUPS[-1][0]) == 0
    h = x.reshape(s, d)
    for layer in range(norm1_gain.shape[0]):
        h = _layer(h, norm1_gain[layer], w_in[layer], b_forget[layer], q_norm_a[layer], k_norm_a[layer],
                   q_norm_b[layer], k_norm_b[layer], w_up_a[layer], w_up_b[layer], w_out[layer], norm2_gain[layer],
                   w_peer_q[layer], peer_subkeys[layer], peer_u[layer], peer_v[layer])
    return h.reshape(b, s, d)
```

```python
import functools

import numpy as np
import jax
import jax.numpy as jnp
from jax import lax
from jax.experimental import pallas as pl
from jax.experimental.pallas import tpu as pltpu

F32 = jnp.float32
BF16 = jnp.bfloat16

D_MODEL = 4096
HEAD_DIM = 128
DILATION_GROUPS = ((128, 1), (512, 4), (2048, 16))
A_SLOTS = 6
N_HEADS_A = A_SLOTS * len(DILATION_GROUPS)
N_HEADS_B = D_MODEL // HEAD_DIM - N_HEADS_A
WIDTH_A = N_HEADS_A * HEAD_DIM
WIDTH_B = N_HEADS_B * HEAD_DIM
WIDTH_A_OUT = A_SLOTS * HEAD_DIM
QKV_COLS = 3 * WIDTH_A + 3 * WIDTH_B
ALIBI_MAX_EXP = 8.0
PEER_HEADS = 8
N_KEYS = 128
N_EXPERTS = N_KEYS * N_KEYS
PEER_TOPK = 16
PEER_QDIM = 256
EPS = 1e-6
NEG = -1e30
LOG2E = 1.4426950408889634

VMEM_LIMIT_BYTES = 52 * 1024 * 1024

NORM_TM = 512
QKV_TM, QKV_TN = 1024, 512
FORGET_TM = 256
FOX_TQ = 512
FOX_ROWS = 16
DIL_T = 256
DIL_ROWS = 8
GATE_TN = 512
UP_TM, UP_TN = 512, 512
OUT_TM, OUT_TN = 1024, 512
PQ_TM = 512
TOPK_TL = 256
PEER_TM, PEER_TE = 512, 512
PEER_LC = 128
FINAL_TM = 256


def _cparams(sem, big=False):
    return pltpu.CompilerParams(dimension_semantics=sem, vmem_limit_bytes=VMEM_LIMIT_BYTES if big else None)


def _dot(a, b):
    return jnp.dot(a, b, preferred_element_type=F32)


def _dot_nt(a, b):
    return lax.dot_general(a, b, (((1,), (1,)), ((), ())), preferred_element_type=F32)


def _rmsnorm_kernel(x_ref, g_ref, o_ref):
    x = x_ref[...]
    ms = jnp.mean(x * x, axis=-1, keepdims=True)
    o_ref[...] = (x * lax.rsqrt(ms + EPS) * g_ref[...]).astype(o_ref.dtype)


def _rmsnorm_bf16(x2d, gain):
    s, d = x2d.shape
    return pl.pallas_call(
        _rmsnorm_kernel,
        grid=(s // NORM_TM,),
        in_specs=[pl.BlockSpec((NORM_TM, d), lambda i: (i, 0)), pl.BlockSpec((1, d), lambda i: (0, 0))],
        out_specs=pl.BlockSpec((NORM_TM, d), lambda i: (i, 0)),
        out_shape=jax.ShapeDtypeStruct((s, d), BF16),
        compiler_params=_cparams(("parallel",)),
        name="rmsnorm",
    )(x2d, gain.reshape(1, d))


def _qkv_kernel(a_ref, w_ref, gain_ref, flag_ref, o_ref, wb_ref):
    @pl.when(pl.program_id(1) == 0)
    def _():
        wb_ref[...] = w_ref[...].astype(BF16)

    acc = _dot(a_ref[...], wb_ref[...])
    for c in range(QKV_TN // HEAD_DIM):
        sl = slice(c * HEAD_DIM, (c + 1) * HEAD_DIM)
        y = acc[:, sl]
        ms = jnp.mean(y * y, axis=-1, keepdims=True)
        yn = y * lax.rsqrt(ms + EPS) * gain_ref[:, sl]
        o_ref[:, sl] = jnp.where(flag_ref[:, sl] > 0.0, yn, y).astype(o_ref.dtype)


def _qkv_proj(xn, w_in, gain, flag):
    s, d = xn.shape
    n = gain.shape[1]
    return pl.pallas_call(
        _qkv_kernel,
        grid=(n // QKV_TN, s // QKV_TM),
        in_specs=[
            pl.BlockSpec((QKV_TM, d), lambda j, i: (i, 0)),
            pl.BlockSpec((d, QKV_TN), lambda j, i: (0, j)),
            pl.BlockSpec((1, QKV_TN), lambda j, i: (0, j)),
            pl.BlockSpec((1, QKV_TN), lambda j, i: (0, j)),
        ],
        out_specs=pl.BlockSpec((QKV_TM, QKV_TN), lambda j, i: (i, j)),
        out_shape=jax.ShapeDtypeStruct((s, n), BF16),
        scratch_shapes=[pltpu.VMEM((d, QKV_TN), BF16)],
        compiler_params=_cparams(("parallel", "arbitrary"), big=True),
        name="qkv_proj",
    )(xn, w_in, gain, flag)


def _forget_kernel(a_ref, w_ref, b_ref, c_ref, ct_ref, carry_ref):
    @pl.when(pl.program_id(0) == 0)
    def _():
        carry_ref[...] = jnp.zeros_like(carry_ref)

    f = _dot(a_ref[...], w_ref[...].astype(BF16)) + b_ref[...]
    logf = jnp.minimum(f, 0.0) - jnp.log1p(jnp.exp(-jnp.abs(f)))
    r = lax.broadcasted_iota(jnp.int32, (FORGET_TM, FORGET_TM), 0)
    c = lax.broadcasted_iota(jnp.int32, (FORGET_TM, FORGET_TM), 1)
    tri = (c <= r).astype(F32)
    cs = jnp.dot(tri, logf, precision=lax.Precision.HIGHEST, preferred_element_type=F32) + carry_ref[...]
    cs2 = cs * LOG2E
    c_ref[...] = cs2
    ct_ref[...] = cs2.T
    carry_ref[...] = cs[FORGET_TM - 1:FORGET_TM, :]


def _forget_cumsum(xn, w_in, b_f):
    s, d = xn.shape
    assert QKV_COLS % HEAD_DIM == 0
    return pl.pallas_call(
        _forget_kernel,
        grid=(s // FORGET_TM,),
        in_specs=[
            pl.BlockSpec((FORGET_TM, d), lambda i: (i, 0)),
            pl.BlockSpec((d, HEAD_DIM), lambda i: (0, QKV_COLS // HEAD_DIM)),
            pl.BlockSpec((1, HEAD_DIM), lambda i: (0, 0)),
        ],
        out_specs=[pl.BlockSpec((FORGET_TM, HEAD_DIM), lambda i: (i, 0)), pl.BlockSpec((HEAD_DIM, FORGET_TM), lambda i: (0, i))],
        out_shape=[jax.ShapeDtypeStruct((s, HEAD_DIM), F32), jax.ShapeDtypeStruct((HEAD_DIM, s), F32)],
        scratch_shapes=[pltpu.VMEM((1, HEAD_DIM), F32)],
        compiler_params=_cparams(("arbitrary",)),
        name="forget_cumsum",
    )(xn, w_in, b_f)


def _alibi_slopes():
    n = N_HEADS_A
    return np.exp2(-np.float32(ALIBI_MAX_EXP) * np.arange(1, n + 1, dtype=np.float32) / np.float32(n)).astype(np.float32)


def _dilated_tiles():
    tiles, first = [], []
    for g, (window, dilation) in enumerate(DILATION_GROUPS):
        first.append(len(tiles))
        for back in range((window + DIL_T - 1) // DIL_T + 1):
            tiles.append((g, dilation, window, back))
    return tiles, first


def _dilated_kernel(slope_ref, q0_ref, q1_ref, q2_ref, k0_ref, k1_ref, k2_ref, v0_ref, v1_ref, v2_ref, y_ref,
                    bm_ref, s0_ref, s1_ref, s2_ref, p0_ref, p1_ref, p2_ref, m_ref, l_ref, acc_ref, *, seq):
    i = pl.program_id(1)
    T = DIL_T
    R = DIL_ROWS
    tiles, first = _dilated_tiles()
    none_tile = len(tiles)
    q_refs = (q0_ref, q1_ref, q2_ref)
    k_refs = (k0_ref, k1_ref, k2_ref)
    v_refs = (v0_ref, v1_ref, v2_ref)
    s_refs = (s0_ref, s1_ref, s2_ref)
    p_refs = (p0_ref, p1_ref, p2_ref)

    @pl.when(i == 0)
    def _():
        base = lax.broadcasted_iota(jnp.int32, (T, T), 0) - lax.broadcasted_iota(jnp.int32, (T, T), 1)
        for idx, (g, dilation, window, back) in enumerate(tiles):
            rel = base + T * back
            ok = jnp.where((base & (dilation - 1)) == 0, rel, -1)
            ok = jnp.where(ok <= window, ok, -1)
            bm_ref[idx] = jnp.where(ok >= 0, -slope_ref[g:g + 1, :] * rel.astype(F32), NEG)
        bm_ref[none_tile] = jnp.full((T, T), NEG, F32)

    for g, (window, dilation) in enumerate(DILATION_GROUPS):
        n_back = (window + T - 1) // T
        n_cols = min(n_back + 1, seq // T)
        first_blk = jnp.clip(i - n_back, 0, seq // T - n_cols)
        strip = pl.ds(pl.multiple_of(first_blk * T, T), n_cols * T)
        s_ref, p_ref = s_refs[g], p_refs[g]
        s_ref[...] = _dot_nt(q_refs[g][...], k_refs[g][strip, :])
        tile_of = []
        for c in range(n_cols):
            back = i - (first_blk + c)
            tile_of.append(jnp.where((back >= 0) & (back <= n_back), first[g] + back, none_tile))
        lane_blocks = [(c, h) for c in range(n_cols) for h in range(T // HEAD_DIM)]
        for rc in range(T // R):
            rows = slice(rc * R, (rc + 1) * R)
            ts = [s_ref[rows, c * T + h * HEAD_DIM:c * T + (h + 1) * HEAD_DIM]
                  + bm_ref[tile_of[c], rows, h * HEAD_DIM:(h + 1) * HEAD_DIM] for c, h in lane_blocks]
            m = jnp.max(functools.reduce(jnp.maximum, ts), axis=-1, keepdims=True)
            ps = [jnp.exp2(t - m) for t in ts]
            l = jnp.sum(functools.reduce(jnp.add, ps), axis=-1, keepdims=True)
            m_ref[g, rows, :] = jnp.broadcast_to(m, (R, HEAD_DIM))
            l_ref[g, rows, :] = jnp.broadcast_to(l, (R, HEAD_DIM))
            for (c, h), p in zip(lane_blocks, ps):
                p_ref[rows, c * T + h * HEAD_DIM:c * T + (h + 1) * HEAD_DIM] = p.astype(BF16)
        acc_ref[g] = _dot(p_ref[...], v_refs[g][strip, :])

    ms = [m_ref[g] for g in range(len(DILATION_GROUPS))]
    m_star = functools.reduce(jnp.maximum, ms)
    num = 0.0
    den = 0.0
    for g, m in enumerate(ms):
        w = jnp.exp2(m - m_star)
        num = num + w * acc_ref[g]
        den = den + w * l_ref[g]
    y_ref[...] = (num / den).astype(y_ref.dtype)


def _dilated_mixture(qkv):
    s = qkv.shape[0]
    T = DIL_T
    n_groups = len(DILATION_GROUPS)
    tiles, _ = _dilated_tiles()
    cols = [min((window + T - 1) // T + 1, s // T) for window, _ in DILATION_GROUPS]
    slopes = (_alibi_slopes() * np.float32(LOG2E)).reshape(n_groups, A_SLOTS).T
    slopes = jnp.asarray(np.broadcast_to(slopes[:, :, None], (A_SLOTS, n_groups, T)).copy())
    k0 = N_HEADS_A
    v0 = 2 * N_HEADS_A
    q_specs = [pl.BlockSpec((T, HEAD_DIM), lambda j, i, g=g: (i, g * A_SLOTS + j)) for g in range(n_groups)]
    k_specs = [pl.BlockSpec((s, HEAD_DIM), lambda j, i, g=g: (0, k0 + g * A_SLOTS + j)) for g in range(n_groups)]
    v_specs = [pl.BlockSpec((s, HEAD_DIM), lambda j, i, g=g: (0, v0 + g * A_SLOTS + j)) for g in range(n_groups)]
    return pl.pallas_call(
        functools.partial(_dilated_kernel, seq=s),
        grid=(A_SLOTS, s // T),
        in_specs=[pl.BlockSpec((None, n_groups, T), lambda j, i: (j, 0, 0))] + q_specs + k_specs + v_specs,
        out_specs=pl.BlockSpec((T, HEAD_DIM), lambda j, i: (i, j)),
        out_shape=jax.ShapeDtypeStruct((s, WIDTH_A_OUT), BF16),
        scratch_shapes=[
            pltpu.VMEM((len(tiles) + 1, T, T), F32),
            *[pltpu.VMEM((T, c * T), F32) for c in cols], *[pltpu.VMEM((T, c * T), BF16) for c in cols],
            pltpu.VMEM((n_groups, T, HEAD_DIM), F32), pltpu.VMEM((n_groups, T, HEAD_DIM), F32),
            pltpu.VMEM((n_groups, T, HEAD_DIM), F32),
        ],
        compiler_params=_cparams(("parallel", "arbitrary"), big=True),
        name="dilated_mixture",
    )(slopes, *([qkv] * (3 * n_groups)))


def _fox_kernel(q_ref, k_ref, v_ref, cq_ref, ck_ref, o_ref, s0_ref, s1_ref, p0_ref, p1_ref, cqr_ref, m_ref, l_ref,
                alpha_ref, acc_ref):
    h = pl.program_id(0)
    qb = pl.program_id(1)
    T = FOX_TQ
    R = FOX_ROWS
    G = T // HEAD_DIM
    s_refs = (s0_ref, s1_ref)
    p_refs = (p0_ref, p1_ref)
    lane = lax.broadcasted_iota(jnp.int32, (T, HEAD_DIM), 1)
    cq = jnp.sum(jnp.where(lane == h, cq_ref[...], 0.0), axis=-1, keepdims=True)
    cqr_ref[...] = jnp.broadcast_to(cq, (T, HEAD_DIM))
    m_ref[...] = jnp.full((T, HEAD_DIM), NEG, F32)
    l_ref[...] = jnp.zeros((T, HEAD_DIM), F32)
    acc_ref[...] = jnp.zeros((T, HEAD_DIM), F32)
    p1_ref[...] = jnp.zeros((T, T), BF16)

    def keys(j):
        return pl.ds(pl.multiple_of(j * T, T), T)

    def logits(j, slot):
        s_refs[slot][...] = _dot_nt(q_ref[...], k_ref[keys(j), :])

    def softmax(j, slot, diagonal):
        s_ref, p_ref = s_refs[slot], p_refs[slot]
        start = pl.multiple_of(j * T, T)
        for rc in range(T // R):
            rows = slice(rc * R, (rc + 1) * R)
            cq_r = cqr_ref[rows, :]
            sg = []
            for g in range(G):
                lo = g * HEAD_DIM
                if diagonal and lo > rc * R + R - 1:
                    sg.append(None)
                    continue
                t = s_ref[rows, lo:lo + HEAD_DIM] + cq_r - ck_ref[:, pl.ds(start + lo, HEAD_DIM)]
                if diagonal and lo + HEAD_DIM - 1 > rc * R:
                    rel = (lax.broadcasted_iota(jnp.int32, (R, HEAD_DIM), 0) - lax.broadcasted_iota(jnp.int32, (R, HEAD_DIM), 1))
                    t = jnp.where(rel >= lo - rc * R, t, NEG)
                sg.append(t)
            live = [t for t in sg if t is not None]
            m_old = m_ref[rows, :]
            m_new = jnp.maximum(m_old, jnp.max(functools.reduce(jnp.maximum, live), axis=-1, keepdims=True))
            alpha = jnp.exp2(m_old - m_new)
            ps = [None if t is None else jnp.exp2(t - m_new) for t in sg]
            row_sum = jnp.sum(functools.reduce(jnp.add, [p for p in ps if p is not None]), axis=-1, keepdims=True)
            l_ref[rows, :] = alpha * l_ref[rows, :] + row_sum
            m_ref[rows, :] = m_new
            alpha_ref[rows, :] = alpha
            for g in range(G):
                lo = g * HEAD_DIM
                p_ref[rows, lo:lo + HEAD_DIM] = (jnp.zeros((R, HEAD_DIM), BF16) if ps[g] is None else ps[g].astype(BF16))

    def stage(j, slot, diagonal):
        other = 1 - slot
        if not diagonal:
            logits(j + 1, other)
        pv = _dot(p_refs[other][...], v_ref[keys(jnp.maximum(j - 1, 0)), :])
        softmax(j, slot, diagonal)
        acc_ref[...] = (acc_ref[...] + pv) * alpha_ref[...]

    logits(0, 0)

    def pair(i, carry):
        stage(2 * i, 0, False)
        stage(2 * i + 1, 1, False)
        return carry

    lax.fori_loop(0, qb // 2, pair, 0)

    def finish(slot):
        stage(qb, slot, True)
        acc = acc_ref[...] + _dot(p_refs[slot][...], v_ref[keys(qb), :])
        o_ref[...] = (acc / l_ref[...]).astype(o_ref.dtype)

    @pl.when(qb % 2 == 0)
    def _():
        finish(0)

    @pl.when(qb % 2 == 1)
    def _():
        stage(qb - 1, 0, False)
        finish(1)


def _forgetting_attention(qkv, c, ct):
    s = qkv.shape[0]
    T = FOX_TQ
    q0 = 3 * WIDTH_A // HEAD_DIM
    k0 = q0 + N_HEADS_B
    v0 = k0 + N_HEADS_B
    ck = ct.reshape(HEAD_DIM, 1, s)
    return pl.pallas_call(
        _fox_kernel,
        grid=(N_HEADS_B, s // T),
        in_specs=[
            pl.BlockSpec((T, HEAD_DIM), lambda h, i: (i, q0 + h)),
            pl.BlockSpec((s, HEAD_DIM), lambda h, i: (0, k0 + h)),
            pl.BlockSpec((s, HEAD_DIM), lambda h, i: (0, v0 + h)),
            pl.BlockSpec((T, HEAD_DIM), lambda h, i: (i, 0)),
            pl.BlockSpec((None, 1, s), lambda h, i: (h, 0, 0)),
        ],
        out_specs=pl.BlockSpec((T, HEAD_DIM), lambda h, i: (i, h)),
        out_shape=jax.ShapeDtypeStruct((s, WIDTH_B), BF16),
        scratch_shapes=[
            pltpu.VMEM((T, T), F32), pltpu.VMEM((T, T), F32), pltpu.VMEM((T, T), BF16), pltpu.VMEM((T, T), BF16),
            pltpu.VMEM((T, HEAD_DIM), F32), pltpu.VMEM((T, HEAD_DIM), F32), pltpu.VMEM((T, HEAD_DIM), F32),
            pltpu.VMEM((T, HEAD_DIM), F32), pltpu.VMEM((T, HEAD_DIM), F32),
        ],
        compiler_params=_cparams(("parallel", "arbitrary")),
        name="fox_attention",
    )(qkv, qkv, qkv, c, ck)


def _gate_cast_kernel(main_ref, extra_ref, o_ref, *, lane_off):
    x = jnp.concatenate([main_ref[...], extra_ref[...]], axis=1)
    o_ref[...] = x[:, lane_off:lane_off + GATE_TN].astype(o_ref.dtype)


def _gate_weights(w_in, g_lo):
    d = w_in.shape[0]
    blk0, lane_off = divmod(g_lo, HEAD_DIM)
    assert (blk0 * HEAD_DIM) % GATE_TN == 0 and g_lo + 2 * d <= w_in.shape[1]
    lanes_per_tile = GATE_TN // HEAD_DIM
    return pl.pallas_call(
        functools.partial(_gate_cast_kernel, lane_off=lane_off),
        grid=(2 * d // GATE_TN,),
        in_specs=[
            pl.BlockSpec((d, GATE_TN), lambda j: (0, blk0 // lanes_per_tile + j)),
            pl.BlockSpec((d, HEAD_DIM), lambda j: (0, blk0 + lanes_per_tile * (j + 1))),
        ],
        out_specs=pl.BlockSpec((d, GATE_TN), lambda j: (0, j)),
        out_shape=jax.ShapeDtypeStruct((d, 2 * d), BF16),
        compiler_params=_cparams(("parallel",), big=True),
        name="gate_weights",
    )(w_in, w_in)


def _up_kernel(xn_ref, ya_ref, yb_ref, wga_ref, wgb_ref, wua_ref, wub_ref, o_ref):
    xn = xn_ref[...]
    ga = jax.nn.sigmoid(_dot(xn, wga_ref[...]))
    ua = _dot(ya_ref[...], wua_ref[...])
    part = ga * ua
    gb = jax.nn.sigmoid(_dot(xn, wgb_ref[...]))
    ub = _dot(yb_ref[...], wub_ref[...])
    o_ref[...] = (part + gb * ub).astype(o_ref.dtype)


def _gated_up(xn, ya, yb, w_gates, w_up_a, w_up_b):
    s, d = xn.shape
    row = lambda width: pl.BlockSpec((UP_TM, width), lambda i, j: (i, 0))
    col = lambda depth: pl.BlockSpec((depth, UP_TN), lambda i, j: (0, j))
    gate_b = pl.BlockSpec((d, UP_TN), lambda i, j: (0, j + d // UP_TN))
    return pl.pallas_call(
        _up_kernel,
        grid=(s // UP_TM, d // UP_TN),
        in_specs=[row(d), row(WIDTH_A_OUT), row(WIDTH_B), col(d), gate_b, col(WIDTH_A_OUT), col(WIDTH_B)],
        out_specs=pl.BlockSpec((UP_TM, UP_TN), lambda i, j: (i, j)),
        out_shape=jax.ShapeDtypeStruct((s, d), BF16),
        compiler_params=_cparams(("parallel", "arbitrary"), big=True),
        name="gated_up",
    )(xn, ya, yb, w_gates, w_gates, w_up_a, w_up_b)


def _out_kernel(a_ref, w_ref, x_ref, o_ref):
    o_ref[...] = x_ref[...] + _dot(a_ref[...], w_ref[...])


def _out_proj(merged, w_out, x2d):
    s, d = x2d.shape
    return pl.pallas_call(
        _out_kernel,
        grid=(s // OUT_TM, d // OUT_TN),
        in_specs=[
            pl.BlockSpec((OUT_TM, d), lambda i, j: (i, 0)),
            pl.BlockSpec((d, OUT_TN), lambda i, j: (0, j)),
            pl.BlockSpec((OUT_TM, OUT_TN), lambda i, j: (i, j)),
        ],
        out_specs=pl.BlockSpec((OUT_TM, OUT_TN), lambda i, j: (i, j)),
        out_shape=jax.ShapeDtypeStruct((s, d), F32),
        compiler_params=_cparams(("parallel", "arbitrary"), big=True),
        name="out_proj",
    )(merged, w_out, x2d)


def _peer_scores_kernel(wq_ref, hn_ref, sk_ref, o_ref):
    qt = _dot_nt(wq_ref[...], hn_ref[...]).astype(BF16)
    half = PEER_QDIM // 2
    for c in range(2):
        o_ref[c * N_KEYS:(c + 1) * N_KEYS, :] = _dot(sk_ref[c * N_KEYS:(c + 1) * N_KEYS, :], qt[c * half:(c + 1) * half, :])


def _peer_scores(hn, wq_t, sk):
    s, d = hn.shape
    rows = PEER_HEADS * 2 * N_KEYS
    return pl.pallas_call(
        _peer_scores_kernel,
        grid=(s // PQ_TM, PEER_HEADS),
        in_specs=[
            pl.BlockSpec((PEER_QDIM, d), lambda i, j: (j, 0)),
            pl.BlockSpec((PQ_TM, d), lambda i, j: (i, 0)),
            pl.BlockSpec((2 * N_KEYS, PEER_QDIM // 2), lambda i, j: (j, 0)),
        ],
        out_specs=pl.BlockSpec((2 * N_KEYS, PQ_TM), lambda i, j: (j, i)),
        out_shape=jax.ShapeDtypeStruct((rows, s), F32),
        compiler_params=_cparams(("parallel", "arbitrary")),
        name="peer_scores",
    )(wq_t, hn, sk)


def _stair_width(a):
    return PEER_TOPK // (a + 1)


def _topk_kernel(sc_ref, e1_ref, w2_ref, tau_ref):
    K = PEER_TOPK
    s1 = sc_ref[0:N_KEYS, :]
    s2 = sc_ref[N_KEYS:2 * N_KEYS, :]
    ninf = -jnp.inf

    def top(cur, count):
        outs = []
        for _ in range(count):
            mk = jnp.max(cur, axis=0, keepdims=True)
            outs.append(mk)
            cur = jnp.where(cur == mk, ninf, cur)
        return outs

    t1 = top(s1, K)
    t2 = top(s2, K)
    t2_all = jnp.concatenate(t2, axis=0)
    rank = lax.broadcasted_iota(jnp.int32, t2_all.shape, 0)
    n_wide = K // 2
    blocks = [jnp.where(rank < _stair_width(a), t1[a] + t2_all, ninf) for a in range(n_wide)]
    blocks.append(jnp.concatenate(t1[n_wide:], axis=0) + t2[0])
    cand = jnp.concatenate(blocks, axis=0)
    thr = top(cand, K)[K - 1]
    z = jnp.sum(jnp.where(cand >= thr, jnp.exp(cand - (t1[0] + t2[0])), 0.0), axis=0, keepdims=True)
    e1_ref[...] = jnp.exp(s1 - t1[0])
    w2_ref[...] = jnp.exp(s2 - t2[0]) / z
    tau = jnp.full(s1.shape, jnp.inf, F32)
    for b in range(K):
        tau = jnp.where(s1 + t2[b] >= thr, t2[b], tau)
    tau_ref[...] = tau


def _peer_select(sc_t):
    s = sc_t.shape[1]
    rows = PEER_HEADS * N_KEYS
    spec = pl.BlockSpec((N_KEYS, TOPK_TL), lambda i, j: (j, i))
    sds = jax.ShapeDtypeStruct((rows, s), F32)
    return pl.pallas_call(
        _topk_kernel,
        grid=(s // TOPK_TL, PEER_HEADS),
        in_specs=[pl.BlockSpec((2 * N_KEYS, TOPK_TL), lambda i, j: (j, i))],
        out_specs=[spec, spec, spec],
        out_shape=[sds, sds, sds],
        compiler_params=_cparams(("parallel", "arbitrary")),
        name="peer_select",
    )(sc_t)


def _peer_kernel(hn_ref, u_ref, vt_ref, s2_ref, tau_ref, e1_ref, w2_ref, o_ref, a0_ref, a1_ref, ag0_ref, ag1_ref):
    ej = pl.program_id(1)
    HALF = PEER_TE // 2
    KEY_BLOCKS = HALF // N_KEYS

    @pl.when(ej == 0)
    def _():
        o_ref[...] = jnp.zeros_like(o_ref)

    inv_sqrt2 = 0.7071067811865476

    def gate(a_ref, ag_ref, half):
        for b in range(KEY_BLOCKS):
            i1 = ej * (PEER_TE // N_KEYS) + half * KEY_BLOCKS + b
            rows = slice(b * N_KEYS, (b + 1) * N_KEYS)
            for lc in range(PEER_TM // PEER_LC):
                cols = slice(lc * PEER_LC, (lc + 1) * PEER_LC)
                g = jnp.zeros((N_KEYS, PEER_LC), F32)
                for h in range(PEER_HEADS):
                    tau = tau_ref[i1, h:h + 1, cols]
                    e1 = e1_ref[i1, h:h + 1, cols]
                    g = g + jnp.where(s2_ref[h, :, cols] >= tau, w2_ref[h * N_KEYS:(h + 1) * N_KEYS, cols], 0.0) * e1
                a = a_ref[rows, cols]
                act = 0.5 * a * (1.0 + lax.erf(a * inv_sqrt2))
                ag_ref[rows, cols] = (act * g).astype(BF16)

    a0_ref[...] = _dot_nt(u_ref[0:HALF, :], hn_ref[...])
    a1_ref[...] = _dot_nt(u_ref[HALF:PEER_TE, :], hn_ref[...])
    gate(a0_ref, ag0_ref, 0)
    gate(a1_ref, ag1_ref, 1)
    o_ref[...] += _dot(vt_ref[:, 0:HALF], ag0_ref[...]) + _dot(vt_ref[:, HALF:PEER_TE], ag1_ref[...])


def _peer_dense(hn, u, vt, sc_t, e1, w2, tau):
    s, d = hn.shape
    once = pl.Buffered(1)
    tau_r = tau.reshape(PEER_HEADS, N_KEYS, s).transpose(1, 0, 2)
    e1_r = e1.reshape(PEER_HEADS, N_KEYS, s).transpose(1, 0, 2)
    by_key = pl.BlockSpec((N_KEYS, PEER_HEADS, PEER_TM), lambda i, j: (0, 0, i), pipeline_mode=once)
    sc4 = sc_t.reshape(PEER_HEADS, 2, N_KEYS, s)
    half = PEER_TE // 2
    return pl.pallas_call(
        _peer_kernel,
        grid=(s // PEER_TM, N_EXPERTS // PEER_TE),
        in_specs=[
            pl.BlockSpec((PEER_TM, d), lambda i, j: (i, 0), pipeline_mode=once),
            pl.BlockSpec((PEER_TE, d), lambda i, j: (j, 0)),
            pl.BlockSpec((d, PEER_TE), lambda i, j: (0, j)),
            pl.BlockSpec((PEER_HEADS, None, N_KEYS, PEER_TM), lambda i, j: (0, 1, 0, i), pipeline_mode=once),
            by_key,
            by_key,
            pl.BlockSpec((PEER_HEADS * N_KEYS, PEER_TM), lambda i, j: (0, i), pipeline_mode=once),
        ],
        out_specs=pl.BlockSpec((d, PEER_TM), lambda i, j: (0, i), pipeline_mode=once),
        out_shape=jax.ShapeDtypeStruct((d, s), F32),
        scratch_shapes=[pltpu.VMEM((half, PEER_TM), F32), pltpu.VMEM((half, PEER_TM), F32),
                        pltpu.VMEM((half, PEER_TM), BF16), pltpu.VMEM((half, PEER_TM), BF16)],
        compiler_params=_cparams(("parallel", "arbitrary"), big=True),
        name="peer_dense",
    )(hn, u, vt, sc4, tau_r, e1_r, w2)


def _final_kernel(h_ref, pt_ref, o_ref):
    o_ref[...] = h_ref[...] + pt_ref[...].T


def _final_add(h, peer_t):
    s, d = h.shape
    return pl.pallas_call(
        _final_kernel,
        grid=(s // FINAL_TM,),
        in_specs=[pl.BlockSpec((FINAL_TM, d), lambda i: (i, 0)), pl.BlockSpec((d, FINAL_TM), lambda i: (0, i))],
        out_specs=pl.BlockSpec((FINAL_TM, d), lambda i: (i, 0)),
        out_shape=jax.ShapeDtypeStruct((s, d), F32),
        compiler_params=_cparams(("parallel",)),
        name="final_add",
    )(h, peer_t)


def _layer(h, norm1_gain, w_in, b_forget, q_norm_a, k_norm_a, q_norm_b, k_norm_b,
           w_up_a, w_up_b, w_out, norm2_gain, w_peer_q, peer_subkeys, peer_u, peer_v):
    d = D_MODEL
    scale = HEAD_DIM ** -0.5
    f_lo = QKV_COLS
    g_lo = f_lo + N_HEADS_B
    b_f = jnp.pad(b_forget.astype(F32), (0, HEAD_DIM - N_HEADS_B)).reshape(1, HEAD_DIM)
    w_gates = _gate_weights(w_in, g_lo)
    ones = jnp.ones((HEAD_DIM,), F32)
    gain = jnp.concatenate([
        jnp.tile(q_norm_a.astype(F32) * (scale * LOG2E), N_HEADS_A), jnp.tile(k_norm_a.astype(F32), N_HEADS_A), jnp.tile(ones, N_HEADS_A),
        jnp.tile(q_norm_b.astype(F32) * (scale * LOG2E), N_HEADS_B), jnp.tile(k_norm_b.astype(F32), N_HEADS_B), jnp.tile(ones, N_HEADS_B),
    ]).reshape(1, QKV_COLS)
    flag = jnp.concatenate([
        jnp.ones((2 * WIDTH_A,), F32), jnp.zeros((WIDTH_A,), F32), jnp.ones((2 * WIDTH_B,), F32), jnp.zeros((WIDTH_B,), F32),
    ]).reshape(1, QKV_COLS)

    xn = _rmsnorm_bf16(h, norm1_gain)
    qkv = _qkv_proj(xn, w_in, gain, flag)
    c, ct = _forget_cumsum(xn, w_in, b_f)
    y_a = _dilated_mixture(qkv)
    y_b = _forgetting_attention(qkv, c, ct)
    merged = _gated_up(xn, y_a, y_b, w_gates, w_up_a.astype(BF16), w_up_b.astype(BF16))
    h = _out_proj(merged, w_out.astype(BF16), h)

    hn = _rmsnorm_bf16(h, norm2_gain)
    wq_t = w_peer_q.T.astype(BF16)
    sk = peer_subkeys.reshape(PEER_HEADS * 2 * N_KEYS, PEER_QDIM // 2).astype(BF16)
    sc_t = _peer_scores(hn, wq_t, sk)
    e1, w2, tau = _peer_select(sc_t)
    peer_t = _peer_dense(hn, peer_u.astype(BF16), peer_v.T.astype(BF16), sc_t, e1, w2, tau)
    return _final_add(h, peer_t)


def kernel(x, norm1_gain, w_in, b_forget, q_norm_a, k_norm_a, q_norm_b, k_norm_b,
           w_up_a, w_up_b, w_out, norm2_gain, w_peer_q, peer_subkeys, peer_u, peer_v):
    b, s, d = x.shape
    assert b == 1 and d == D_MODEL and s % (DILATION_GROUPS[-1][0]) == 0
    h = x.reshape(s, d)
    for layer in range(norm1_gain.shape[0]):
        h = _layer(h, norm1_gain[layer], w_in[layer], b_forget[layer], q_norm_a[layer], k_norm_a[layer],
                   q_norm_b[layer], k_norm_b[layer], w_up_a[layer], w_up_b[layer], w_out[layer], norm2_gain[layer],
                   w_peer_q[layer], peer_subkeys[layer], peer_u[layer], peer_v[layer])
    return h.reshape(b, s, d)
```

```python
import functools

import numpy as np
import jax
import jax.numpy as jnp
from jax import lax
from jax.experimental import pallas as pl
from jax.experimental.pallas import tpu as pltpu

F32 = jnp.float32
BF16 = jnp.bfloat16

D_MODEL = 4096
HEAD_DIM = 128
DILATION_GROUPS = ((128, 1), (512, 4), (2048, 16))
A_SLOTS = 6
N_HEADS_A = A_SLOTS * len(DILATION_GROUPS)
N_HEADS_B = D_MODEL // HEAD_DIM - N_HEADS_A
WIDTH_A = N_HEADS_A * HEAD_DIM
WIDTH_B = N_HEADS_B * HEAD_DIM
WIDTH_A_OUT = A_SLOTS * HEAD_DIM
QKV_COLS = 3 * WIDTH_A + 3 * WIDTH_B
ALIBI_MAX_EXP = 8.0
PEER_HEADS = 8
N_KEYS = 128
N_EXPERTS = N_KEYS * N_KEYS
PEER_TOPK = 16
PEER_QDIM = 256
EPS = 1e-6
NEG = -1e30
LOG2E = 1.4426950408889634

VMEM_LIMIT_BYTES = 52 * 1024 * 1024

NORM_TM = 512
QKV_TM, QKV_TN = 1024, 512
FORGET_TM = 256
FOX_TQ = 512
FOX_ROWS = 16
DIL_T = 256
DIL_ROWS = 8
GATE_TN = 512
UP_TM, UP_TN = 512, 512
OUT_TM, OUT_TN = 1024, 512
PQ_TM = 512
TOPK_TL = 256
PEER_TM, PEER_TE = 512, 512
PEER_LC = 128
FINAL_TM = 256


def _cparams(sem, big=False):
    return pltpu.CompilerParams(dimension_semantics=sem, vmem_limit_bytes=VMEM_LIMIT_BYTES if big else None)


def _dot(a, b):
    return jnp.dot(a, b, preferred_element_type=F32)


def _dot_nt(a, b):
    return lax.dot_general(a, b, (((1,), (1,)), ((), ())), preferred_element_type=F32)


def _rmsnorm_kernel(x_ref, g_ref, o_ref):
    x = x_ref[...]
    ms = jnp.mean(x * x, axis=-1, keepdims=True)
    o_ref[...] = (x * lax.rsqrt(ms + EPS) * g_ref[...]).astype(o_ref.dtype)


def _rmsnorm_bf16(x2d, gain):
    s, d = x2d.shape
    return pl.pallas_call(
        _rmsnorm_kernel,
        grid=(s // NORM_TM,),
        in_specs=[pl.BlockSpec((NORM_TM, d), lambda i: (i, 0)), pl.BlockSpec((1, d), lambda i: (0, 0))],
        out_specs=pl.BlockSpec((NORM_TM, d), lambda i: (i, 0)),
        out_shape=jax.ShapeDtypeStruct((s, d), BF16),
        compiler_params=_cparams(("parallel",)),
        name="rmsnorm",
    )(x2d, gain.reshape(1, d))


def _qkv_kernel(a_ref, w_ref, gain_ref, flag_ref, o_ref, wb_ref):
    @pl.when(pl.program_id(1) == 0)
    def _():
        wb_ref[...] = w_ref[...].astype(BF16)

    acc = _dot_nt(a_ref[...], wb_ref[...])
    for c in range(QKV_TN // HEAD_DIM):
        sl = slice(c * HEAD_DIM, (c + 1) * HEAD_DIM)
        y = acc[:, sl]
        ms = jnp.mean(y * y, axis=-1, keepdims=True)
        yn = y * lax.rsqrt(ms + EPS) * gain_ref[:, sl]
        o_ref[:, sl] = jnp.where(flag_ref[:, sl] > 0.0, yn, y).astype(o_ref.dtype)


def _qkv_proj(xn, w_in_t, gain, flag):
    s, d = xn.shape
    n = gain.shape[1]
    return pl.pallas_call(
        _qkv_kernel,
        grid=(n // QKV_TN, s // QKV_TM),
        in_specs=[
            pl.BlockSpec((QKV_TM, d), lambda j, i: (i, 0)),
            pl.BlockSpec((QKV_TN, d), lambda j, i: (j, 0)),
            pl.BlockSpec((1, QKV_TN), lambda j, i: (0, j)),
            pl.BlockSpec((1, QKV_TN), lambda j, i: (0, j)),
        ],
        out_specs=pl.BlockSpec((QKV_TM, QKV_TN), lambda j, i: (i, j)),
        out_shape=jax.ShapeDtypeStruct((s, n), BF16),
        scratch_shapes=[pltpu.VMEM((QKV_TN, d), BF16)],
        compiler_params=_cparams(("parallel", "arbitrary"), big=True),
        name="qkv_proj",
    )(xn, w_in_t, gain, flag)


def _forget_kernel(a_ref, w_ref, b_ref, c_ref, ct_ref, carry_ref):
    @pl.when(pl.program_id(0) == 0)
    def _():
        carry_ref[...] = jnp.zeros_like(carry_ref)

    f = _dot_nt(a_ref[...], w_ref[...].astype(BF16)) + b_ref[...]
    logf = jnp.minimum(f, 0.0) - jnp.log1p(jnp.exp(-jnp.abs(f)))
    r = lax.broadcasted_iota(jnp.int32, (FORGET_TM, FORGET_TM), 0)
    c = lax.broadcasted_iota(jnp.int32, (FORGET_TM, FORGET_TM), 1)
    tri = (c <= r).astype(F32)
    cs = jnp.dot(tri, logf, precision=lax.Precision.HIGHEST, preferred_element_type=F32) + carry_ref[...]
    cs2 = cs * LOG2E
    c_ref[...] = cs2
    ct_ref[...] = cs2.T
    carry_ref[...] = cs[FORGET_TM - 1:FORGET_TM, :]


def _forget_cumsum(xn, w_in_t, b_f):
    s, d = xn.shape
    assert QKV_COLS % HEAD_DIM == 0
    return pl.pallas_call(
        _forget_kernel,
        grid=(s // FORGET_TM,),
        in_specs=[
            pl.BlockSpec((FORGET_TM, d), lambda i: (i, 0)),
            pl.BlockSpec((HEAD_DIM, d), lambda i: (QKV_COLS // HEAD_DIM, 0)),
            pl.BlockSpec((1, HEAD_DIM), lambda i: (0, 0)),
        ],
        out_specs=[pl.BlockSpec((FORGET_TM, HEAD_DIM), lambda i: (i, 0)), pl.BlockSpec((HEAD_DIM, FORGET_TM), lambda i: (0, i))],
        out_shape=[jax.ShapeDtypeStruct((s, HEAD_DIM), F32), jax.ShapeDtypeStruct((HEAD_DIM, s), F32)],
        scratch_shapes=[pltpu.VMEM((1, HEAD_DIM), F32)],
        compiler_params=_cparams(("arbitrary",)),
        name="forget_cumsum",
    )(xn, w_in_t, b_f)


def _alibi_slopes():
    n = N_HEADS_A
    return np.exp2(-np.float32(ALIBI_MAX_EXP) * np.arange(1, n + 1, dtype=np.float32) / np.float32(n)).astype(np.float32)


def _dilated_tiles():
    tiles, first = [], []
    for g, (window, dilation) in enumerate(DILATION_GROUPS):
        first.append(len(tiles))
        for back in range((window + DIL_T - 1) // DIL_T + 1):
            tiles.append((g, dilation, window, back))
    return tiles, first


def _dilated_kernel(slope_ref, q0_ref, q1_ref, q2_ref, k0_ref, k1_ref, k2_ref, v0_ref, v1_ref, v2_ref, y_ref,
                    bm_ref, s0_ref, s1_ref, s2_ref, p0_ref, p1_ref, p2_ref, m_ref, l_ref, acc_ref, *, seq):
    i = pl.program_id(1)
    T = DIL_T
    R = DIL_ROWS
    tiles, first = _dilated_tiles()
    none_tile = len(tiles)
    q_refs = (q0_ref, q1_ref, q2_ref)
    k_refs = (k0_ref, k1_ref, k2_ref)
    v_refs = (v0_ref, v1_ref, v2_ref)
    s_refs = (s0_ref, s1_ref, s2_ref)
    p_refs = (p0_ref, p1_ref, p2_ref)

    @pl.when(i == 0)
    def _():
        base = lax.broadcasted_iota(jnp.int32, (T, T), 0) - lax.broadcasted_iota(jnp.int32, (T, T), 1)
        for idx, (g, dilation, window, back) in enumerate(tiles):
            rel = base + T * back
            ok = jnp.where((base & (dilation - 1)) == 0, rel, -1)
            ok = jnp.where(ok <= window, ok, -1)
            bm_ref[idx] = jnp.where(ok >= 0, -slope_ref[g:g + 1, :] * rel.astype(F32), NEG)
        bm_ref[none_tile] = jnp.full((T, T), NEG, F32)

    for g, (window, dilation) in enumerate(DILATION_GROUPS):
        n_back = (window + T - 1) // T
        n_cols = min(n_back + 1, seq // T)
        first_blk = jnp.clip(i - n_back, 0, seq // T - n_cols)
        strip = pl.ds(pl.multiple_of(first_blk * T, T), n_cols * T)
        s_ref, p_ref = s_refs[g], p_refs[g]
        s_ref[...] = _dot_nt(q_refs[g][...], k_refs[g][strip, :])
        tile_of = []
        for c in range(n_cols):
            back = i - (first_blk + c)
            tile_of.append(jnp.where((back >= 0) & (back <= n_back), first[g] + back, none_tile))
        lane_blocks = [(c, h) for c in range(n_cols) for h in range(T // HEAD_DIM)]
        for rc in range(T // R):
            rows = slice(rc * R, (rc + 1) * R)
            ts = [s_ref[rows, c * T + h * HEAD_DIM:c * T + (h + 1) * HEAD_DIM]
                  + bm_ref[tile_of[c], rows, h * HEAD_DIM:(h + 1) * HEAD_DIM] for c, h in lane_blocks]
            m = jnp.max(functools.reduce(jnp.maximum, ts), axis=-1, keepdims=True)
            ps = [jnp.exp2(t - m) for t in ts]
            l = jnp.sum(functools.reduce(jnp.add, ps), axis=-1, keepdims=True)
            m_ref[g, rows, :] = jnp.broadcast_to(m, (R, HEAD_DIM))
            l_ref[g, rows, :] = jnp.broadcast_to(l, (R, HEAD_DIM))
            for (c, h), p in zip(lane_blocks, ps):
                p_ref[rows, c * T + h * HEAD_DIM:c * T + (h + 1) * HEAD_DIM] = p.astype(BF16)
        acc_ref[g] = _dot(p_ref[...], v_refs[g][strip, :])

    ms = [m_ref[g] for g in range(len(DILATION_GROUPS))]
    m_star = functools.reduce(jnp.maximum, ms)
    num = 0.0
    den = 0.0
    for g, m in enumerate(ms):
        w = jnp.exp2(m - m_star)
        num = num + w * acc_ref[g]
        den = den + w * l_ref[g]
    y_ref[...] = (num / den).astype(y_ref.dtype)


def _dilated_mixture(qkv):
    s = qkv.shape[0]
    T = DIL_T
    n_groups = len(DILATION_GROUPS)
    tiles, _ = _dilated_tiles()
    cols = [min((window + T - 1) // T + 1, s // T) for window, _ in DILATION_GROUPS]
    slopes = (_alibi_slopes() * np.float32(LOG2E)).reshape(n_groups, A_SLOTS).T
    slopes = jnp.asarray(np.broadcast_to(slopes[:, :, None], (A_SLOTS, n_groups, T)).copy())
    k0 = N_HEADS_A
    v0 = 2 * N_HEADS_A
    q_specs = [pl.BlockSpec((T, HEAD_DIM), lambda j, i, g=g: (i, g * A_SLOTS + j)) for g in range(n_groups)]
    k_specs = [pl.BlockSpec((s, HEAD_DIM), lambda j, i, g=g: (0, k0 + g * A_SLOTS + j)) for g in range(n_groups)]
    v_specs = [pl.BlockSpec((s, HEAD_DIM), lambda j, i, g=g: (0, v0 + g * A_SLOTS + j)) for g in range(n_groups)]
    return pl.pallas_call(
        functools.partial(_dilated_kernel, seq=s),
        grid=(A_SLOTS, s // T),
        in_specs=[pl.BlockSpec((None, n_groups, T), lambda j, i: (j, 0, 0))] + q_specs + k_specs + v_specs,
        out_specs=pl.BlockSpec((T, HEAD_DIM), lambda j, i: (i, j)),
        out_shape=jax.ShapeDtypeStruct((s, WIDTH_A_OUT), BF16),
        scratch_shapes=[
            pltpu.VMEM((len(tiles) + 1, T, T), F32),
            *[pltpu.VMEM((T, c * T), F32) for c in cols], *[pltpu.VMEM((T, c * T), BF16) for c in cols],
            pltpu.VMEM((n_groups, T, HEAD_DIM), F32), pltpu.VMEM((n_groups, T, HEAD_DIM), F32),
            pltpu.VMEM((n_groups, T, HEAD_DIM), F32),
        ],
        compiler_params=_cparams(("parallel", "arbitrary"), big=True),
        name="dilated_mixture",
    )(slopes, *([qkv] * (3 * n_groups)))


def _fox_kernel(q_ref, k_ref, v_ref, cq_ref, ck_ref, o_ref, s0_ref, s1_ref, p0_ref, p1_ref, cqr_ref, m_ref, l_ref,
                alpha_ref, acc_ref):
    h = pl.program_id(0)
    qb = pl.program_id(1)
    T = FOX_TQ
    R = FOX_ROWS
    G = T // HEAD_DIM
    s_refs = (s0_ref, s1_ref)
    p_refs = (p0_ref, p1_ref)
    lane = lax.broadcasted_iota(jnp.int32, (T, HEAD_DIM), 1)
    cq = jnp.sum(jnp.where(lane == h, cq_ref[...], 0.0), axis=-1, keepdims=True)
    cqr_ref[...] = jnp.broadcast_to(cq, (T, HEAD_DIM))
    m_ref[...] = jnp.full((T, HEAD_DIM), NEG, F32)
    l_ref[...] = jnp.zeros((T, HEAD_DIM), F32)
    acc_ref[...] = jnp.zeros((T, HEAD_DIM), F32)
    p1_ref[...] = jnp.zeros((T, T), BF16)

    def keys(j):
        return pl.ds(pl.multiple_of(j * T, T), T)

    def logits(j, slot):
        s_refs[slot][...] = _dot_nt(q_ref[...], k_ref[keys(j), :])

    def softmax(j, slot, diagonal):
        s_ref, p_ref = s_refs[slot], p_refs[slot]
        start = pl.multiple_of(j * T, T)
        for rc in range(T // R):
            rows = slice(rc * R, (rc + 1) * R)
            cq_r = cqr_ref[rows, :]
            sg = []
            for g in range(G):
                lo = g * HEAD_DIM
                if diagonal and lo > rc * R + R - 1:
                    sg.append(None)
                    continue
                t = s_ref[rows, lo:lo + HEAD_DIM] + cq_r - ck_ref[:, pl.ds(start + lo, HEAD_DIM)]
                if diagonal and lo + HEAD_DIM - 1 > rc * R:
                    rel = (lax.broadcasted_iota(jnp.int32, (R, HEAD_DIM), 0) - lax.broadcasted_iota(jnp.int32, (R, HEAD_DIM), 1))
                    t = jnp.where(rel >= lo - rc * R, t, NEG)
                sg.append(t)
            live = [t for t in sg if t is not None]
            m_old = m_ref[rows, :]
            m_new = jnp.maximum(m_old, jnp.max(functools.reduce(jnp.maximum, live), axis=-1, keepdims=True))
            alpha = jnp.exp2(m_old - m_new)
            ps = [None if t is None else jnp.exp2(t - m_new) for t in sg]
            row_sum = jnp.sum(functools.reduce(jnp.add, [p for p in ps if p is not None]), axis=-1, keepdims=True)
            l_ref[rows, :] = alpha * l_ref[rows, :] + row_sum
            m_ref[rows, :] = m_new
            alpha_ref[rows, :] = alpha
            for g in range(G):
                lo = g * HEAD_DIM
                p_ref[rows, lo:lo + HEAD_DIM] = (jnp.zeros((R, HEAD_DIM), BF16) if ps[g] is None else ps[g].astype(BF16))

    def stage(j, slot, diagonal):
        other = 1 - slot
        if not diagonal:
            logits(j + 1, other)
        pv = _dot(p_refs[other][...], v_ref[keys(jnp.maximum(j - 1, 0)), :])
        softmax(j, slot, diagonal)
        acc_ref[...] = (acc_ref[...] + pv) * alpha_ref[...]

    logits(0, 0)

    def pair(i, carry):
        stage(2 * i, 0, False)
        stage(2 * i + 1, 1, False)
        return carry

    lax.fori_loop(0, qb // 2, pair, 0)

    def finish(slot):
        stage(qb, slot, True)
        acc = acc_ref[...] + _dot(p_refs[slot][...], v_ref[keys(qb), :])
        o_ref[...] = (acc / l_ref[...]).astype(o_ref.dtype)

    @pl.when(qb % 2 == 0)
    def _():
        finish(0)

    @pl.when(qb % 2 == 1)
    def _():
        stage(qb - 1, 0, False)
        finish(1)


def _forgetting_attention(qkv, c, ct):
    s = qkv.shape[0]
    T = FOX_TQ
    q0 = 3 * WIDTH_A // HEAD_DIM
    k0 = q0 + N_HEADS_B
    v0 = k0 + N_HEADS_B
    ck = ct.reshape(HEAD_DIM, 1, s)
    return pl.pallas_call(
        _fox_kernel,
        grid=(N_HEADS_B, s // T),
        in_specs=[
            pl.BlockSpec((T, HEAD_DIM), lambda h, i: (i, q0 + h)),
            pl.BlockSpec((s, HEAD_DIM), lambda h, i: (0, k0 + h)),
            pl.BlockSpec((s, HEAD_DIM), lambda h, i: (0, v0 + h)),
            pl.BlockSpec((T, HEAD_DIM), lambda h, i: (i, 0)),
            pl.BlockSpec((None, 1, s), lambda h, i: (h, 0, 0)),
        ],
        out_specs=pl.BlockSpec((T, HEAD_DIM), lambda h, i: (i, h)),
        out_shape=jax.ShapeDtypeStruct((s, WIDTH_B), BF16),
        scratch_shapes=[
            pltpu.VMEM((T, T), F32), pltpu.VMEM((T, T), F32), pltpu.VMEM((T, T), BF16), pltpu.VMEM((T, T), BF16),
            pltpu.VMEM((T, HEAD_DIM), F32), pltpu.VMEM((T, HEAD_DIM), F32), pltpu.VMEM((T, HEAD_DIM), F32),
            pltpu.VMEM((T, HEAD_DIM), F32), pltpu.VMEM((T, HEAD_DIM), F32),
        ],
        compiler_params=_cparams(("parallel", "arbitrary")),
        name="fox_attention",
    )(qkv, qkv, qkv, c, ck)


def _gate_cast_kernel(main_ref, extra_ref, o_ref, *, row_off):
    x = jnp.concatenate([main_ref[...], extra_ref[...]], axis=0)
    o_ref[...] = x[row_off:row_off + GATE_TN, :].astype(o_ref.dtype)


def _gate_weights(w_in_t, g_lo):
    d = w_in_t.shape[1]
    sub = 8
    row0, row_off = (g_lo // sub) * sub, g_lo % sub
    assert g_lo + 2 * d <= w_in_t.shape[0] and row0 + 2 * d + sub <= w_in_t.shape[0] + sub
    return pl.pallas_call(
        functools.partial(_gate_cast_kernel, row_off=row_off),
        grid=(2 * d // GATE_TN,),
        in_specs=[
            pl.BlockSpec((pl.Element(GATE_TN), pl.Element(d)), lambda j: (pl.multiple_of(row0 + j * GATE_TN, sub), 0)),
            pl.BlockSpec((pl.Element(sub), pl.Element(d)), lambda j: (pl.multiple_of(row0 + (j + 1) * GATE_TN, sub), 0)),
        ],
        out_specs=pl.BlockSpec((GATE_TN, d), lambda j: (j, 0)),
        out_shape=jax.ShapeDtypeStruct((2 * d, d), BF16),
        compiler_params=_cparams(("parallel",), big=True),
        name="gate_weights",
    )(w_in_t, w_in_t)


def _up_kernel(xn_ref, ya_ref, yb_ref, wga_ref, wgb_ref, wua_ref, wub_ref, o_ref):
    xn = xn_ref[...]
    ga = jax.nn.sigmoid(_dot_nt(xn, wga_ref[...]))
    ua = _dot(ya_ref[...], wua_ref[...])
    part = ga * ua
    gb = jax.nn.sigmoid(_dot_nt(xn, wgb_ref[...]))
    ub = _dot(yb_ref[...], wub_ref[...])
    o_ref[...] = (part + gb * ub).astype(o_ref.dtype)


def _gated_up(xn, ya, yb, w_gates, w_up_a, w_up_b):
    s, d = xn.shape
    row = lambda width: pl.BlockSpec((UP_TM, width), lambda i, j: (i, 0))
    col = lambda depth: pl.BlockSpec((depth, UP_TN), lambda i, j: (0, j))
    gate_a = pl.BlockSpec((UP_TN, d), lambda i, j: (j, 0))
    gate_b = pl.BlockSpec((UP_TN, d), lambda i, j: (j + d // UP_TN, 0))
    return pl.pallas_call(
        _up_kernel,
        grid=(s // UP_TM, d // UP_TN),
        in_specs=[row(d), row(WIDTH_A_OUT), row(WIDTH_B), gate_a, gate_b, col(WIDTH_A_OUT), col(WIDTH_B)],
        out_specs=pl.BlockSpec((UP_TM, UP_TN), lambda i, j: (i, j)),
        out_shape=jax.ShapeDtypeStruct((s, d), BF16),
        compiler_params=_cparams(("parallel", "arbitrary"), big=True),
        name="gated_up",
    )(xn, ya, yb, w_gates, w_gates, w_up_a, w_up_b)


def _out_kernel(a_ref, w_ref, x_ref, o_ref):
    o_ref[...] = x_ref[...] + _dot(a_ref[...], w_ref[...])


def _out_proj(merged, w_out, x2d):
    s, d = x2d.shape
    return pl.pallas_call(
        _out_kernel,
        grid=(s // OUT_TM, d // OUT_TN),
        in_specs=[
            pl.BlockSpec((OUT_TM, d), lambda i, j: (i, 0)),
            pl.BlockSpec((d, OUT_TN), lambda i, j: (0, j)),
            pl.BlockSpec((OUT_TM, OUT_TN), lambda i, j: (i, j)),
        ],
        out_specs=pl.BlockSpec((OUT_TM, OUT_TN), lambda i, j: (i, j)),
        out_shape=jax.ShapeDtypeStruct((s, d), F32),
        compiler_params=_cparams(("parallel", "arbitrary"), big=True),
        name="out_proj",
    )(merged, w_out, x2d)


def _peer_scores_kernel(wq_ref, hn_ref, sk_ref, o_ref):
    qt = _dot_nt(wq_ref[...], hn_ref[...]).astype(BF16)
    half = PEER_QDIM // 2
    for c in range(2):
        o_ref[c * N_KEYS:(c + 1) * N_KEYS, :] = _dot(sk_ref[c * N_KEYS:(c + 1) * N_KEYS, :], qt[c * half:(c + 1) * half, :])


def _peer_scores(hn, wq_t, sk):
    s, d = hn.shape
    rows = PEER_HEADS * 2 * N_KEYS
    return pl.pallas_call(
        _peer_scores_kernel,
        grid=(s // PQ_TM, PEER_HEADS),
        in_specs=[
            pl.BlockSpec((PEER_QDIM, d), lambda i, j: (j, 0)),
            pl.BlockSpec((PQ_TM, d), lambda i, j: (i, 0)),
            pl.BlockSpec((2 * N_KEYS, PEER_QDIM // 2), lambda i, j: (j, 0)),
        ],
        out_specs=pl.BlockSpec((2 * N_KEYS, PQ_TM), lambda i, j: (j, i)),
        out_shape=jax.ShapeDtypeStruct((rows, s), F32),
        compiler_params=_cparams(("parallel", "arbitrary")),
        name="peer_scores",
    )(wq_t, hn, sk)


def _stair_width(a):
    return PEER_TOPK // (a + 1)


def _topk_kernel(sc_ref, e1_ref, w2_ref, tau_ref):
    K = PEER_TOPK
    s1 = sc_ref[0:N_KEYS, :]
    s2 = sc_ref[N_KEYS:2 * N_KEYS, :]
    ninf = -jnp.inf

    def top(cur, count):
        outs = []
        for _ in range(count):
            mk = jnp.max(cur, axis=0, keepdims=True)
            outs.append(mk)
            cur = jnp.where(cur == mk, ninf, cur)
        return outs

    t1 = top(s1, K)
    t2 = top(s2, K)
    t2_all = jnp.concatenate(t2, axis=0)
    rank = lax.broadcasted_iota(jnp.int32, t2_all.shape, 0)
    n_wide = K // 2
    blocks = [jnp.where(rank < _stair_width(a), t1[a] + t2_all, ninf) for a in range(n_wide)]
    blocks.append(jnp.concatenate(t1[n_wide:], axis=0) + t2[0])
    cand = jnp.concatenate(blocks, axis=0)
    thr = top(cand, K)[K - 1]
    z = jnp.sum(jnp.where(cand >= thr, jnp.exp(cand - (t1[0] + t2[0])), 0.0), axis=0, keepdims=True)
    e1_ref[...] = jnp.exp(s1 - t1[0])
    w2_ref[...] = jnp.exp(s2 - t2[0]) / z
    tau = jnp.full(s1.shape, jnp.inf, F32)
    for b in range(K):
        tau = jnp.where(s1 + t2[b] >= thr, t2[b], tau)
    tau_ref[...] = tau


def _peer_select(sc_t):
    s = sc_t.shape[1]
    rows = PEER_HEADS * N_KEYS
    spec = pl.BlockSpec((N_KEYS, TOPK_TL), lambda i, j: (j, i))
    sds = jax.ShapeDtypeStruct((rows, s), F32)
    return pl.pallas_call(
        _topk_kernel,
        grid=(s // TOPK_TL, PEER_HEADS),
        in_specs=[pl.BlockSpec((2 * N_KEYS, TOPK_TL), lambda i, j: (j, i))],
        out_specs=[spec, spec, spec],
        out_shape=[sds, sds, sds],
        compiler_params=_cparams(("parallel", "arbitrary")),
        name="peer_select",
    )(sc_t)


def _peer_kernel(hn_ref, u_ref, vt_ref, s2_ref, tau_ref, e1_ref, w2_ref, o_ref, a0_ref, a1_ref, ag0_ref, ag1_ref):
    ej = pl.program_id(1)
    HT = PEER_TM // 2
    keys_per_tile = PEER_TE // N_KEYS
    lane_chunks = HT // PEER_LC
    n_chunks = keys_per_tile * lane_chunks
    d = u_ref.shape[1]
    kc_size = d // n_chunks
    row_size = d // n_chunks

    @pl.when(ej == 0)
    def _():
        o_ref[...] = jnp.zeros_like(o_ref)

    inv_sqrt2 = 0.7071067811865476

    def gate_chunk(a_ref, ag_ref, half, c):
        b, lc = divmod(c, lane_chunks)
        i1 = ej * keys_per_tile + b
        rows = slice(b * N_KEYS, (b + 1) * N_KEYS)
        cols = slice(lc * PEER_LC, (lc + 1) * PEER_LC)
        tok = slice(half * HT + lc * PEER_LC, half * HT + (lc + 1) * PEER_LC)
        g = jnp.zeros((N_KEYS, PEER_LC), F32)
        for h in range(PEER_HEADS):
            tau = tau_ref[i1, h:h + 1, tok]
            e1 = e1_ref[i1, h:h + 1, tok]
            g = g + jnp.where(s2_ref[h, :, tok] >= tau, w2_ref[h * N_KEYS:(h + 1) * N_KEYS, tok], 0.0) * e1
        a = a_ref[rows, cols]
        act = 0.5 * a * (1.0 + lax.erf(a * inv_sqrt2))
        ag_ref[rows, cols] = (act * g).astype(BF16)

    a0_ref[...] = _dot_nt(u_ref[...], hn_ref[0:HT, :])
    for c in range(n_chunks):
        kc = slice(c * kc_size, (c + 1) * kc_size)
        piece = _dot_nt(u_ref[:, kc], hn_ref[HT:PEER_TM, kc])
        if c == 0:
            a1_ref[...] = piece
        else:
            a1_ref[...] += piece
        gate_chunk(a0_ref, ag0_ref, 0, c)
    for c in range(n_chunks):
        rows = slice(c * row_size, (c + 1) * row_size)
        o_ref[rows, 0:HT] += _dot(vt_ref[rows, :], ag0_ref[...])
        gate_chunk(a1_ref, ag1_ref, 1, c)
    o_ref[:, HT:PEER_TM] += _dot(vt_ref[...], ag1_ref[...])


def _peer_dense(hn, u, vt, sc_t, e1, w2, tau):
    s, d = hn.shape
    once = pl.Buffered(1)
    tau_r = tau.reshape(PEER_HEADS, N_KEYS, s).transpose(1, 0, 2)
    e1_r = e1.reshape(PEER_HEADS, N_KEYS, s).transpose(1, 0, 2)
    by_key = pl.BlockSpec((N_KEYS, PEER_HEADS, PEER_TM), lambda i, j: (0, 0, i), pipeline_mode=once)
    sc4 = sc_t.reshape(PEER_HEADS, 2, N_KEYS, s)
    half = PEER_TM // 2
    return pl.pallas_call(
        _peer_kernel,
        grid=(s // PEER_TM, N_EXPERTS // PEER_TE),
        in_specs=[
            pl.BlockSpec((PEER_TM, d), lambda i, j: (i, 0), pipeline_mode=once),
            pl.BlockSpec((PEER_TE, d), lambda i, j: (j, 0)),
            pl.BlockSpec((d, PEER_TE), lambda i, j: (0, j)),
            pl.BlockSpec((PEER_HEADS, None, N_KEYS, PEER_TM), lambda i, j: (0, 1, 0, i), pipeline_mode=once),
            by_key,
            by_key,
            pl.BlockSpec((PEER_HEADS * N_KEYS, PEER_TM), lambda i, j: (0, i), pipeline_mode=once),
        ],
        out_specs=pl.BlockSpec((d, PEER_TM), lambda i, j: (0, i), pipeline_mode=once),
        out_shape=jax.ShapeDtypeStruct((d, s), F32),
        scratch_shapes=[pltpu.VMEM((PEER_TE, half), F32), pltpu.VMEM((PEER_TE, half), F32),
                        pltpu.VMEM((PEER_TE, half), BF16), pltpu.VMEM((PEER_TE, half), BF16)],
        compiler_params=_cparams(("parallel", "arbitrary"), big=True),
        name="peer_dense",
    )(hn, u, vt, sc4, tau_r, e1_r, w2)


def _final_kernel(h_ref, pt_ref, o_ref):
    o_ref[...] = h_ref[...] + pt_ref[...].T


def _final_add(h, peer_t):
    s, d = h.shape
    return pl.pallas_call(
        _final_kernel,
        grid=(s // FINAL_TM,),
        in_specs=[pl.BlockSpec((FINAL_TM, d), lambda i: (i, 0)), pl.BlockSpec((d, FINAL_TM), lambda i: (0, i))],
        out_specs=pl.BlockSpec((FINAL_TM, d), lambda i: (i, 0)),
        out_shape=jax.ShapeDtypeStruct((s, d), F32),
        compiler_params=_cparams(("parallel",)),
        name="final_add",
    )(h, peer_t)


def _layer(h, norm1_gain, w_in, b_forget, q_norm_a, k_norm_a, q_norm_b, k_norm_b,
           w_up_a, w_up_b, w_out, norm2_gain, w_peer_q, peer_subkeys, peer_u, peer_v):
    d = D_MODEL
    scale = HEAD_DIM ** -0.5
    f_lo = QKV_COLS
    g_lo = f_lo + N_HEADS_B
    b_f = jnp.pad(b_forget.astype(F32), (0, HEAD_DIM - N_HEADS_B)).reshape(1, HEAD_DIM)
    w_in_t = w_in.T
    w_gates = _gate_weights(w_in_t, g_lo)
    ones = jnp.ones((HEAD_DIM,), F32)
    gain = jnp.concatenate([
        jnp.tile(q_norm_a.astype(F32) * (scale * LOG2E), N_HEADS_A), jnp.tile(k_norm_a.astype(F32), N_HEADS_A), jnp.tile(ones, N_HEADS_A),
        jnp.tile(q_norm_b.astype(F32) * (scale * LOG2E), N_HEADS_B), jnp.tile(k_norm_b.astype(F32), N_HEADS_B), jnp.tile(ones, N_HEADS_B),
    ]).reshape(1, QKV_COLS)
    flag = jnp.concatenate([
        jnp.ones((2 * WIDTH_A,), F32), jnp.zeros((WIDTH_A,), F32), jnp.ones((2 * WIDTH_B,), F32), jnp.zeros((WIDTH_B,), F32),
    ]).reshape(1, QKV_COLS)

    xn = _rmsnorm_bf16(h, norm1_gain)
    qkv = _qkv_proj(xn, w_in_t, gain, flag)
    c, ct = _forget_cumsum(xn, w_in_t, b_f)
    y_a = _dilated_mixture(qkv)
    y_b = _forgetting_attention(qkv, c, ct)
    merged = _gated_up(xn, y_a, y_b, w_gates, w_up_a.astype(BF16), w_up_b.astype(BF16))
    h = _out_proj(merged, w_out.astype(BF16), h)

    hn = _rmsnorm_bf16(h, norm2_gain)
    wq_t = w_peer_q.T.astype(BF16)
    sk = peer_subkeys.reshape(PEER_HEADS * 2 * N_KEYS, PEER_QDIM // 2).astype(BF16)
    sc_t = _peer_scores(hn, wq_t, sk)
    e1, w2, tau = _peer_select(sc_t)
    peer_t = _peer_dense(hn, peer_u.astype(BF16), peer_v.T.astype(BF16), sc_t, e1, w2, tau)
    return _final_add(h, peer_t)


def kernel(x, norm1_gain, w_in, b_forget, q_norm_a, k_norm_a, q_norm_b, k_norm_b,
           w_up_a, w_up_b, w_out, norm2_gain, w_peer_q, peer_subkeys, peer_u, peer_v):
    b, s, d = x.shape
    assert b == 1 and d == D_MODEL and s % (DILATION_GROUPS[-1][0]) == 0
    h = x.reshape(s, d)
    for layer in range(norm1_gain.shape[0]):
        h = _layer(h, norm1_gain[layer], w_in[layer], b_forget[layer], q_norm_a[layer], k_norm_a[layer],
                   q_norm_b[layer], k_norm_b[layer], w_up_a[layer], w_up_b[layer], w_out[layer], norm2_gain[layer],
                   w_peer_q[layer], peer_subkeys[layer], peer_u[layer], peer_v[layer])
    return h.reshape(b, s, d)
```

```python
import functools

import numpy as np
import jax
import jax.numpy as jnp
from jax import lax
from jax.experimental import pallas as pl
from jax.experimental.pallas import tpu as pltpu

F32 = jnp.float32
BF16 = jnp.bfloat16

D_MODEL = 4096
HEAD_DIM = 128
DILATION_GROUPS = ((128, 1), (512, 4), (2048, 16))
A_SLOTS = 6
N_HEADS_A = A_SLOTS * len(DILATION_GROUPS)
N_HEADS_B = D_MODEL // HEAD_DIM - N_HEADS_A
WIDTH_A = N_HEADS_A * HEAD_DIM
WIDTH_B = N_HEADS_B * HEAD_DIM
WIDTH_A_OUT = A_SLOTS * HEAD_DIM
QKV_COLS = 3 * WIDTH_A + 3 * WIDTH_B
ALIBI_MAX_EXP = 8.0
PEER_HEADS = 8
N_KEYS = 128
N_EXPERTS = N_KEYS * N_KEYS
PEER_TOPK = 16
PEER_QDIM = 256
EPS = 1e-6
NEG = -1e30
LOG2E = 1.4426950408889634

VMEM_LIMIT_BYTES = 56 * 1024 * 1024

NORM_TM = 512
QKV_TM, QKV_TN = 1024, 512
FORGET_TM = 256
FOX_TQ = 512
FOX_ROWS = 16
DIL_T = 256
DIL_ROWS = 8
GATE_TN, GATE_PAD = 512, 16
UP_TM, UP_TN = 512, 512
OUT_TM, OUT_TN = 1024, 512
PQ_TM = 512
TOPK_TL = 256
PEER_TM, PEER_TE = 512, 512
PEER_LC = 128


def _cparams(sem, big=False):
    return pltpu.CompilerParams(dimension_semantics=sem, vmem_limit_bytes=VMEM_LIMIT_BYTES if big else None)


def _dot(a, b):
    return jnp.dot(a, b, preferred_element_type=F32)


def _dot_nt(a, b):
    return lax.dot_general(a, b, (((1,), (1,)), ((), ())), preferred_element_type=F32)


def _rmsnorm_kernel(x_ref, g_ref, o_ref):
    x = x_ref[...]
    ms = jnp.mean(x * x, axis=-1, keepdims=True)
    o_ref[...] = (x * lax.rsqrt(ms + EPS) * g_ref[...]).astype(o_ref.dtype)


def _rmsnorm_bf16(x2d, gain):
    s, d = x2d.shape
    return pl.pallas_call(
        _rmsnorm_kernel,
        grid=(s // NORM_TM,),
        in_specs=[pl.BlockSpec((NORM_TM, d), lambda i: (i, 0)), pl.BlockSpec((1, d), lambda i: (0, 0))],
        out_specs=pl.BlockSpec((NORM_TM, d), lambda i: (i, 0)),
        out_shape=jax.ShapeDtypeStruct((s, d), BF16),
        compiler_params=_cparams(("parallel",)),
        name="rmsnorm",
    )(x2d, gain.reshape(1, d))


def _qkv_kernel(a_ref, w_ref, gain_ref, flag_ref, o_ref, wb_ref):
    @pl.when(pl.program_id(1) == 0)
    def _():
        wb_ref[...] = w_ref[...].astype(BF16)

    acc = _dot_nt(a_ref[...], wb_ref[...])
    for c in range(QKV_TN // HEAD_DIM):
        sl = slice(c * HEAD_DIM, (c + 1) * HEAD_DIM)
        y = acc[:, sl]
        ms = jnp.mean(y * y, axis=-1, keepdims=True)
        yn = y * lax.rsqrt(ms + EPS) * gain_ref[:, sl]
        o_ref[:, sl] = jnp.where(flag_ref[:, sl] > 0.0, yn, y).astype(o_ref.dtype)


def _qkv_proj(xn, w_in_t, gain, flag):
    s, d = xn.shape
    n = gain.shape[1]
    return pl.pallas_call(
        _qkv_kernel,
        grid=(n // QKV_TN, s // QKV_TM),
        in_specs=[
            pl.BlockSpec((QKV_TM, d), lambda j, i: (i, 0)),
            pl.BlockSpec((QKV_TN, d), lambda j, i: (j, 0)),
            pl.BlockSpec((1, QKV_TN), lambda j, i: (0, j)),
            pl.BlockSpec((1, QKV_TN), lambda j, i: (0, j)),
        ],
        out_specs=pl.BlockSpec((QKV_TM, QKV_TN), lambda j, i: (i, j)),
        out_shape=jax.ShapeDtypeStruct((s, n), BF16),
        scratch_shapes=[pltpu.VMEM((QKV_TN, d), BF16)],
        compiler_params=_cparams(("parallel", "arbitrary"), big=True),
        name="qkv_proj",
    )(xn, w_in_t, gain, flag)


def _forget_kernel(a_ref, w_ref, b_ref, c_ref, ct_ref, carry_ref):
    @pl.when(pl.program_id(0) == 0)
    def _():
        carry_ref[...] = jnp.zeros_like(carry_ref)

    f = _dot_nt(a_ref[...], w_ref[...].astype(BF16)) + b_ref[...]
    logf = jnp.minimum(f, 0.0) - jnp.log1p(jnp.exp(-jnp.abs(f)))
    r = lax.broadcasted_iota(jnp.int32, (FORGET_TM, FORGET_TM), 0)
    c = lax.broadcasted_iota(jnp.int32, (FORGET_TM, FORGET_TM), 1)
    tri = (c <= r).astype(F32)
    cs = jnp.dot(tri, logf, precision=lax.Precision.HIGHEST, preferred_element_type=F32) + carry_ref[...]
    cs2 = cs * LOG2E
    c_ref[...] = cs2
    ct_ref[...] = cs2.T
    carry_ref[...] = cs[FORGET_TM - 1:FORGET_TM, :]


def _forget_cumsum(xn, w_in_t, b_f):
    s, d = xn.shape
    assert QKV_COLS % HEAD_DIM == 0
    return pl.pallas_call(
        _forget_kernel,
        grid=(s // FORGET_TM,),
        in_specs=[
            pl.BlockSpec((FORGET_TM, d), lambda i: (i, 0)),
            pl.BlockSpec((HEAD_DIM, d), lambda i: (QKV_COLS // HEAD_DIM, 0)),
            pl.BlockSpec((1, HEAD_DIM), lambda i: (0, 0)),
        ],
        out_specs=[pl.BlockSpec((FORGET_TM, HEAD_DIM), lambda i: (i, 0)), pl.BlockSpec((HEAD_DIM, FORGET_TM), lambda i: (0, i))],
        out_shape=[jax.ShapeDtypeStruct((s, HEAD_DIM), F32), jax.ShapeDtypeStruct((HEAD_DIM, s), F32)],
        scratch_shapes=[pltpu.VMEM((1, HEAD_DIM), F32)],
        compiler_params=_cparams(("arbitrary",)),
        name="forget_cumsum",
    )(xn, w_in_t, b_f)


def _alibi_slopes():
    n = N_HEADS_A
    return np.exp2(-np.float32(ALIBI_MAX_EXP) * np.arange(1, n + 1, dtype=np.float32) / np.float32(n)).astype(np.float32)


def _dilated_tiles():
    tiles, first = [], []
    for g, (window, dilation) in enumerate(DILATION_GROUPS):
        first.append(len(tiles))
        for back in range((window + DIL_T - 1) // DIL_T + 1):
            tiles.append((g, dilation, window, back))
    return tiles, first


def _dilated_kernel(slope_ref, q0_ref, q1_ref, q2_ref, k0_ref, k1_ref, k2_ref, v0_ref, v1_ref, v2_ref, y_ref,
                    bm_ref, s0_ref, s1_ref, s2_ref, p0_ref, p1_ref, p2_ref, m_ref, l_ref, acc_ref, *, seq):
    i = pl.program_id(1)
    T = DIL_T
    R = DIL_ROWS
    tiles, first = _dilated_tiles()
    none_tile = len(tiles)
    q_refs = (q0_ref, q1_ref, q2_ref)
    k_refs = (k0_ref, k1_ref, k2_ref)
    v_refs = (v0_ref, v1_ref, v2_ref)
    s_refs = (s0_ref, s1_ref, s2_ref)
    p_refs = (p0_ref, p1_ref, p2_ref)

    @pl.when(i == 0)
    def _():
        base = lax.broadcasted_iota(jnp.int32, (T, T), 0) - lax.broadcasted_iota(jnp.int32, (T, T), 1)
        for idx, (g, dilation, window, back) in enumerate(tiles):
            rel = base + T * back
            ok = jnp.where((base & (dilation - 1)) == 0, rel, -1)
            ok = jnp.where(ok <= window, ok, -1)
            bm_ref[idx] = jnp.where(ok >= 0, -slope_ref[g:g + 1, :] * rel.astype(F32), NEG)
        bm_ref[none_tile] = jnp.full((T, T), NEG, F32)

    for g, (window, dilation) in enumerate(DILATION_GROUPS):
        n_back = (window + T - 1) // T
        n_cols = min(n_back + 1, seq // T)
        first_blk = jnp.clip(i - n_back, 0, seq // T - n_cols)
        strip = pl.ds(pl.multiple_of(first_blk * T, T), n_cols * T)
        s_ref, p_ref = s_refs[g], p_refs[g]
        s_ref[...] = _dot_nt(q_refs[g][...], k_refs[g][strip, :])
        tile_of = []
        for c in range(n_cols):
            back = i - (first_blk + c)
            tile_of.append(jnp.where((back >= 0) & (back <= n_back), first[g] + back, none_tile))
        lane_blocks = [(c, h) for c in range(n_cols) for h in range(T // HEAD_DIM)]
        for rc in range(T // R):
            rows = slice(rc * R, (rc + 1) * R)
            ts = [s_ref[rows, c * T + h * HEAD_DIM:c * T + (h + 1) * HEAD_DIM]
                  + bm_ref[tile_of[c], rows, h * HEAD_DIM:(h + 1) * HEAD_DIM] for c, h in lane_blocks]
            m = jnp.max(functools.reduce(jnp.maximum, ts), axis=-1, keepdims=True)
            ps = [jnp.exp2(t - m) for t in ts]
            l = jnp.sum(functools.reduce(jnp.add, ps), axis=-1, keepdims=True)
            m_ref[g, rows, :] = jnp.broadcast_to(m, (R, HEAD_DIM))
            l_ref[g, rows, :] = jnp.broadcast_to(l, (R, HEAD_DIM))
            for (c, h), p in zip(lane_blocks, ps):
                p_ref[rows, c * T + h * HEAD_DIM:c * T + (h + 1) * HEAD_DIM] = p.astype(BF16)
        acc_ref[g] = _dot(p_ref[...], v_refs[g][strip, :])

    ms = [m_ref[g] for g in range(len(DILATION_GROUPS))]
    m_star = functools.reduce(jnp.maximum, ms)
    num = 0.0
    den = 0.0
    for g, m in enumerate(ms):
        w = jnp.exp2(m - m_star)
        num = num + w * acc_ref[g]
        den = den + w * l_ref[g]
    y_ref[...] = (num / den).astype(y_ref.dtype)


def _dilated_mixture(qkv):
    s = qkv.shape[0]
    T = DIL_T
    n_groups = len(DILATION_GROUPS)
    tiles, _ = _dilated_tiles()
    cols = [min((window + T - 1) // T + 1, s // T) for window, _ in DILATION_GROUPS]
    slopes = (_alibi_slopes() * np.float32(LOG2E)).reshape(n_groups, A_SLOTS).T
    slopes = jnp.asarray(np.broadcast_to(slopes[:, :, None], (A_SLOTS, n_groups, T)).copy())
    k0 = N_HEADS_A
    v0 = 2 * N_HEADS_A
    q_specs = [pl.BlockSpec((T, HEAD_DIM), lambda j, i, g=g: (i, g * A_SLOTS + j)) for g in range(n_groups)]
    k_specs = [pl.BlockSpec((s, HEAD_DIM), lambda j, i, g=g: (0, k0 + g * A_SLOTS + j)) for g in range(n_groups)]
    v_specs = [pl.BlockSpec((s, HEAD_DIM), lambda j, i, g=g: (0, v0 + g * A_SLOTS + j)) for g in range(n_groups)]
    return pl.pallas_call(
        functools.partial(_dilated_kernel, seq=s),
        grid=(A_SLOTS, s // T),
        in_specs=[pl.BlockSpec((None, n_groups, T), lambda j, i: (j, 0, 0))] + q_specs + k_specs + v_specs,
        out_specs=pl.BlockSpec((T, HEAD_DIM), lambda j, i: (i, j)),
        out_shape=jax.ShapeDtypeStruct((s, WIDTH_A_OUT), BF16),
        scratch_shapes=[
            pltpu.VMEM((len(tiles) + 1, T, T), F32),
            *[pltpu.VMEM((T, c * T), F32) for c in cols], *[pltpu.VMEM((T, c * T), BF16) for c in cols],
            pltpu.VMEM((n_groups, T, HEAD_DIM), F32), pltpu.VMEM((n_groups, T, HEAD_DIM), F32),
            pltpu.VMEM((n_groups, T, HEAD_DIM), F32),
        ],
        compiler_params=_cparams(("parallel", "arbitrary"), big=True),
        name="dilated_mixture",
    )(slopes, *([qkv] * (3 * n_groups)))


def _fox_kernel(q_ref, k_ref, v_ref, cq_ref, ck_ref, o_ref, s0_ref, s1_ref, p0_ref, p1_ref, cqr_ref, m_ref, l_ref,
                alpha_ref, acc_ref):
    h = pl.program_id(0)
    qb = pl.program_id(1)
    T = FOX_TQ
    R = FOX_ROWS
    G = T // HEAD_DIM
    s_refs = (s0_ref, s1_ref)
    p_refs = (p0_ref, p1_ref)
    lane = lax.broadcasted_iota(jnp.int32, (T, HEAD_DIM), 1)
    cq = jnp.sum(jnp.where(lane == h, cq_ref[...], 0.0), axis=-1, keepdims=True)
    cqr_ref[...] = jnp.broadcast_to(cq, (T, HEAD_DIM))
    m_ref[...] = jnp.full((T, HEAD_DIM), NEG, F32)
    l_ref[...] = jnp.zeros((T, HEAD_DIM), F32)
    acc_ref[...] = jnp.zeros((T, HEAD_DIM), F32)
    p1_ref[...] = jnp.zeros((T, T), BF16)

    def keys(j):
        return pl.ds(pl.multiple_of(j * T, T), T)

    def logits(j, slot):
        s_refs[slot][...] = _dot_nt(q_ref[...], k_ref[keys(j), :])

    def softmax(j, slot, diagonal):
        s_ref, p_ref = s_refs[slot], p_refs[slot]
        start = pl.multiple_of(j * T, T)
        for rc in range(T // R):
            rows = slice(rc * R, (rc + 1) * R)
            cq_r = cqr_ref[rows, :]
            sg = []
            for g in range(G):
                lo = g * HEAD_DIM
                if diagonal and lo > rc * R + R - 1:
                    sg.append(None)
                    continue
                t = s_ref[rows, lo:lo + HEAD_DIM] + cq_r - ck_ref[:, pl.ds(start + lo, HEAD_DIM)]
                if diagonal and lo + HEAD_DIM - 1 > rc * R:
                    rel = (lax.broadcasted_iota(jnp.int32, (R, HEAD_DIM), 0) - lax.broadcasted_iota(jnp.int32, (R, HEAD_DIM), 1))
                    t = jnp.where(rel >= lo - rc * R, t, NEG)
                sg.append(t)
            live = [t for t in sg if t is not None]
            m_old = m_ref[rows, :]
            m_new = jnp.maximum(m_old, jnp.max(functools.reduce(jnp.maximum, live), axis=-1, keepdims=True))
            alpha = jnp.exp2(m_old - m_new)
            ps = [None if t is None else jnp.exp2(t - m_new) for t in sg]
            row_sum = jnp.sum(functools.reduce(jnp.add, [p for p in ps if p is not None]), axis=-1, keepdims=True)
            l_ref[rows, :] = alpha * l_ref[rows, :] + row_sum
            m_ref[rows, :] = m_new
            alpha_ref[rows, :] = alpha
            for g in range(G):
                lo = g * HEAD_DIM
                p_ref[rows, lo:lo + HEAD_DIM] = (jnp.zeros((R, HEAD_DIM), BF16) if ps[g] is None else ps[g].astype(BF16))

    def stage(j, slot, diagonal):
        other = 1 - slot
        if not diagonal:
            logits(j + 1, other)
        pv = _dot(p_refs[other][...], v_ref[keys(jnp.maximum(j - 1, 0)), :])
        softmax(j, slot, diagonal)
        acc_ref[...] = (acc_ref[...] + pv) * alpha_ref[...]

    logits(0, 0)

    def pair(i, carry):
        stage(2 * i, 0, False)
        stage(2 * i + 1, 1, False)
        return carry

    lax.fori_loop(0, qb // 2, pair, 0)

    def finish(slot):
        stage(qb, slot, True)
        acc = acc_ref[...] + _dot(p_refs[slot][...], v_ref[keys(qb), :])
        o_ref[...] = (acc / l_ref[...]).astype(o_ref.dtype)

    @pl.when(qb % 2 == 0)
    def _():
        finish(0)

    @pl.when(qb % 2 == 1)
    def _():
        stage(qb - 1, 0, False)
        finish(1)


def _forgetting_attention(qkv, c, ct):
    s = qkv.shape[0]
    T = FOX_TQ
    q0 = 3 * WIDTH_A // HEAD_DIM
    k0 = q0 + N_HEADS_B
    v0 = k0 + N_HEADS_B
    ck = ct.reshape(HEAD_DIM, 1, s)
    return pl.pallas_call(
        _fox_kernel,
        grid=(N_HEADS_B, s // T),
        in_specs=[
            pl.BlockSpec((T, HEAD_DIM), lambda h, i: (i, q0 + h)),
            pl.BlockSpec((s, HEAD_DIM), lambda h, i: (0, k0 + h)),
            pl.BlockSpec((s, HEAD_DIM), lambda h, i: (0, v0 + h)),
            pl.BlockSpec((T, HEAD_DIM), lambda h, i: (i, 0)),
            pl.BlockSpec((None, 1, s), lambda h, i: (h, 0, 0)),
        ],
        out_specs=pl.BlockSpec((T, HEAD_DIM), lambda h, i: (i, h)),
        out_shape=jax.ShapeDtypeStruct((s, WIDTH_B), BF16),
        scratch_shapes=[
            pltpu.VMEM((T, T), F32), pltpu.VMEM((T, T), F32), pltpu.VMEM((T, T), BF16), pltpu.VMEM((T, T), BF16),
            pltpu.VMEM((T, HEAD_DIM), F32), pltpu.VMEM((T, HEAD_DIM), F32), pltpu.VMEM((T, HEAD_DIM), F32),
            pltpu.VMEM((T, HEAD_DIM), F32), pltpu.VMEM((T, HEAD_DIM), F32),
        ],
        compiler_params=_cparams(("parallel", "arbitrary")),
        name="fox_attention",
    )(qkv, qkv, qkv, c, ck)


def _gate_cast_kernel(main_ref, extra_ref, o_ref, *, row_off):
    x = jnp.concatenate([main_ref[...], extra_ref[...]], axis=0)
    o_ref[...] = x[row_off:row_off + GATE_TN, :].astype(o_ref.dtype)


def _gate_weights(w_in_t, g_lo):
    d = w_in_t.shape[1]
    blk0, row_off = divmod(g_lo, GATE_TN)
    assert row_off <= GATE_PAD and GATE_TN % GATE_PAD == 0 and g_lo + 2 * d <= w_in_t.shape[0]
    return pl.pallas_call(
        functools.partial(_gate_cast_kernel, row_off=row_off),
        grid=(2 * d // GATE_TN,),
        in_specs=[
            pl.BlockSpec((GATE_TN, d), lambda j: (blk0 + j, 0)),
            pl.BlockSpec((GATE_PAD, d), lambda j: ((blk0 + j + 1) * (GATE_TN // GATE_PAD), 0)),
        ],
        out_specs=pl.BlockSpec((GATE_TN, d), lambda j: (j, 0)),
        out_shape=jax.ShapeDtypeStruct((2 * d, d), BF16),
        compiler_params=_cparams(("parallel",), big=True),
        name="gate_weights",
    )(w_in_t, w_in_t)


def _up_kernel(xn_ref, ya_ref, yb_ref, wga_ref, wgb_ref, wua_ref, wub_ref, o_ref):
    xn = xn_ref[...]
    ga = jax.nn.sigmoid(_dot_nt(xn, wga_ref[...]))
    ua = _dot(ya_ref[...], wua_ref[...])
    part = ga * ua
    gb = jax.nn.sigmoid(_dot_nt(xn, wgb_ref[...]))
    ub = _dot(yb_ref[...], wub_ref[...])
    o_ref[...] = (part + gb * ub).astype(o_ref.dtype)


def _gated_up(xn, ya, yb, w_gates, w_up_a, w_up_b):
    s, d = xn.shape
    row = lambda width: pl.BlockSpec((UP_TM, width), lambda i, j: (i, 0))
    col = lambda depth: pl.BlockSpec((depth, UP_TN), lambda i, j: (0, j))
    gate_a = pl.BlockSpec((UP_TN, d), lambda i, j: (j, 0))
    gate_b = pl.BlockSpec((UP_TN, d), lambda i, j: (j + d // UP_TN, 0))
    return pl.pallas_call(
        _up_kernel,
        grid=(s // UP_TM, d // UP_TN),
        in_specs=[row(d), row(WIDTH_A_OUT), row(WIDTH_B), gate_a, gate_b, col(WIDTH_A_OUT), col(WIDTH_B)],
        out_specs=pl.BlockSpec((UP_TM, UP_TN), lambda i, j: (i, j)),
        out_shape=jax.ShapeDtypeStruct((s, d), BF16),
        compiler_params=_cparams(("parallel", "arbitrary"), big=True),
        name="gated_up",
    )(xn, ya, yb, w_gates, w_gates, w_up_a, w_up_b)


def _out_kernel(a_ref, w_ref, x_ref, o_ref):
    o_ref[...] = x_ref[...] + _dot(a_ref[...], w_ref[...])


def _out_proj(merged, w_out, x2d):
    s, d = x2d.shape
    return pl.pallas_call(
        _out_kernel,
        grid=(s // OUT_TM, d // OUT_TN),
        in_specs=[
            pl.BlockSpec((OUT_TM, d), lambda i, j: (i, 0)),
            pl.BlockSpec((d, OUT_TN), lambda i, j: (0, j)),
            pl.BlockSpec((OUT_TM, OUT_TN), lambda i, j: (i, j)),
        ],
        out_specs=pl.BlockSpec((OUT_TM, OUT_TN), lambda i, j: (i, j)),
        out_shape=jax.ShapeDtypeStruct((s, d), F32),
        compiler_params=_cparams(("parallel", "arbitrary"), big=True),
        name="out_proj",
    )(merged, w_out, x2d)


def _peer_scores_kernel(wq_ref, hn_ref, sk_ref, o_ref):
    qt = _dot_nt(wq_ref[...], hn_ref[...]).astype(BF16)
    for r in range(PEER_HEADS * 2):
        rows = slice(r * N_KEYS, (r + 1) * N_KEYS)
        o_ref[rows, :] = _dot(sk_ref[rows, :], qt[rows, :])


def _peer_scores(hn, wq_t, sk):
    s, d = hn.shape
    rows = PEER_HEADS * 2 * N_KEYS
    assert PEER_QDIM // 2 == N_KEYS
    once = pl.Buffered(1)
    return pl.pallas_call(
        _peer_scores_kernel,
        grid=(s // PQ_TM,),
        in_specs=[
            pl.BlockSpec((PEER_HEADS * PEER_QDIM, d), lambda i: (0, 0), pipeline_mode=once),
            pl.BlockSpec((PQ_TM, d), lambda i: (i, 0)),
            pl.BlockSpec((rows, PEER_QDIM // 2), lambda i: (0, 0), pipeline_mode=once),
        ],
        out_specs=pl.BlockSpec((rows, PQ_TM), lambda i: (0, i)),
        out_shape=jax.ShapeDtypeStruct((rows, s), F32),
        compiler_params=_cparams(("parallel",), big=True),
        name="peer_scores",
    )(wq_t, hn, sk)


def _stair_width(a):
    return PEER_TOPK // (a + 1)


def _topk_kernel(sc_ref, e1_ref, w2_ref, tau_ref):
    K = PEER_TOPK
    s1 = sc_ref[0:N_KEYS, :]
    s2 = sc_ref[N_KEYS:2 * N_KEYS, :]
    ninf = -jnp.inf

    def top(cur, count):
        outs = []
        for _ in range(count):
            mk = jnp.max(cur, axis=0, keepdims=True)
            outs.append(mk)
            cur = jnp.where(cur == mk, ninf, cur)
        return outs

    t1 = top(s1, K)
    t2 = top(s2, K)
    t2_all = jnp.concatenate(t2, axis=0)
    rank = lax.broadcasted_iota(jnp.int32, t2_all.shape, 0)
    n_wide = K // 2
    blocks = [jnp.where(rank < _stair_width(a), t1[a] + t2_all, ninf) for a in range(n_wide)]
    blocks.append(jnp.concatenate(t1[n_wide:], axis=0) + t2[0])
    cand = jnp.concatenate(blocks, axis=0)
    thr = top(cand, K)[K - 1]
    z = jnp.sum(jnp.where(cand >= thr, jnp.exp(cand - (t1[0] + t2[0])), 0.0), axis=0, keepdims=True)
    e1_ref[...] = jnp.exp(s1 - t1[0])
    w2_ref[...] = jnp.exp(s2 - t2[0]) / z
    tau = jnp.full(s1.shape, jnp.inf, F32)
    for b in range(K):
        tau = jnp.where(s1 + t2[b] >= thr, t2[b], tau)
    tau_ref[...] = tau


def _peer_select(sc_t):
    s = sc_t.shape[1]
    rows = PEER_HEADS * N_KEYS
    spec = pl.BlockSpec((N_KEYS, TOPK_TL), lambda i, j: (j, i))
    sds = jax.ShapeDtypeStruct((rows, s), F32)
    return pl.pallas_call(
        _topk_kernel,
        grid=(s // TOPK_TL, PEER_HEADS),
        in_specs=[pl.BlockSpec((2 * N_KEYS, TOPK_TL), lambda i, j: (j, i))],
        out_specs=[spec, spec, spec],
        out_shape=[sds, sds, sds],
        compiler_params=_cparams(("parallel", "arbitrary")),
        name="peer_select",
    )(sc_t)


def _peer_kernel(hn_ref, u_ref, vt_ref, s2_ref, tau_ref, e1_ref, w2_ref, h_ref, o_ref, acc_ref, a0_ref, a1_ref,
                 ag0_ref, ag1_ref):
    ej = pl.program_id(1)
    HT = PEER_TM // 2
    keys_per_tile = PEER_TE // N_KEYS
    lane_chunks = HT // PEER_LC
    n_chunks = keys_per_tile * lane_chunks
    d = u_ref.shape[1]
    kc_size = d // n_chunks
    row_size = d // n_chunks

    @pl.when(ej == 0)
    def _():
        acc_ref[...] = jnp.zeros_like(acc_ref)

    inv_sqrt2 = 0.7071067811865476

    def gate_chunk(a_ref, ag_ref, half, c):
        b, lc = divmod(c, lane_chunks)
        i1 = ej * keys_per_tile + b
        rows = slice(b * N_KEYS, (b + 1) * N_KEYS)
        cols = slice(lc * PEER_LC, (lc + 1) * PEER_LC)
        tok = slice(half * HT + lc * PEER_LC, half * HT + (lc + 1) * PEER_LC)
        g = jnp.zeros((N_KEYS, PEER_LC), F32)
        for h in range(PEER_HEADS):
            tau = tau_ref[i1, h:h + 1, tok]
            e1 = e1_ref[i1, h:h + 1, tok]
            g = g + jnp.where(s2_ref[h, :, tok] >= tau, w2_ref[h * N_KEYS:(h + 1) * N_KEYS, tok], 0.0) * e1
        a = a_ref[rows, cols]
        act = 0.5 * a * (1.0 + lax.erf(a * inv_sqrt2))
        ag_ref[rows, cols] = (act * g).astype(BF16)

    a0_ref[...] = _dot_nt(u_ref[...], hn_ref[0:HT, :])
    for c in range(n_chunks):
        kc = slice(c * kc_size, (c + 1) * kc_size)
        piece = _dot_nt(u_ref[:, kc], hn_ref[HT:PEER_TM, kc])
        if c == 0:
            a1_ref[...] = piece
        else:
            a1_ref[...] += piece
        gate_chunk(a0_ref, ag0_ref, 0, c)
    for c in range(n_chunks):
        rows = slice(c * row_size, (c + 1) * row_size)
        acc_ref[rows, 0:HT] += _dot(vt_ref[rows, :], ag0_ref[...])
        gate_chunk(a1_ref, ag1_ref, 1, c)
    acc_ref[:, HT:PEER_TM] += _dot(vt_ref[...], ag1_ref[...])

    @pl.when(ej == pl.num_programs(1) - 1)
    def _():
        for c in range(d // PEER_TM):
            cols = slice(c * PEER_TM, (c + 1) * PEER_TM)
            o_ref[:, cols] = h_ref[:, cols] + acc_ref[cols, :].T


def _peer_dense(hn, u, vt, sc_t, e1, w2, tau, h):
    s, d = hn.shape
    once = pl.Buffered(1)
    tau_r = tau.reshape(PEER_HEADS, N_KEYS, s).transpose(1, 0, 2)
    e1_r = e1.reshape(PEER_HEADS, N_KEYS, s).transpose(1, 0, 2)
    by_key = pl.BlockSpec((N_KEYS, PEER_HEADS, PEER_TM), lambda i, j: (0, 0, i), pipeline_mode=once)
    sc4 = sc_t.reshape(PEER_HEADS, 2, N_KEYS, s)
    half = PEER_TM // 2
    return pl.pallas_call(
        _peer_kernel,
        grid=(s // PEER_TM, N_EXPERTS // PEER_TE),
        in_specs=[
            pl.BlockSpec((PEER_TM, d), lambda i, j: (i, 0), pipeline_mode=once),
            pl.BlockSpec((PEER_TE, d), lambda i, j: (j, 0)),
            pl.BlockSpec((d, PEER_TE), lambda i, j: (0, j)),
            pl.BlockSpec((PEER_HEADS, None, N_KEYS, PEER_TM), lambda i, j: (0, 1, 0, i), pipeline_mode=once),
            by_key,
            by_key,
            pl.BlockSpec((PEER_HEADS * N_KEYS, PEER_TM), lambda i, j: (0, i), pipeline_mode=once),
            pl.BlockSpec((PEER_TM, d), lambda i, j: (i, 0), pipeline_mode=once),
        ],
        out_specs=pl.BlockSpec((PEER_TM, d), lambda i, j: (i, 0), pipeline_mode=once),
        out_shape=jax.ShapeDtypeStruct((s, d), F32),
        scratch_shapes=[pltpu.VMEM((d, PEER_TM), F32),
                        pltpu.VMEM((PEER_TE, half), F32), pltpu.VMEM((PEER_TE, half), F32),
                        pltpu.VMEM((PEER_TE, half), BF16), pltpu.VMEM((PEER_TE, half), BF16)],
        compiler_params=_cparams(("parallel", "arbitrary"), big=True),
        name="peer_dense",
    )(hn, u, vt, sc4, tau_r, e1_r, w2, h)


def _layer(h, norm1_gain, w_in, b_forget, q_norm_a, k_norm_a, q_norm_b, k_norm_b,
           w_up_a, w_up_b, w_out, norm2_gain, w_peer_q, peer_subkeys, peer_u, peer_v):
    d = D_MODEL
    scale = HEAD_DIM ** -0.5
    f_lo = QKV_COLS
    g_lo = f_lo + N_HEADS_B
    b_f = jnp.pad(b_forget.astype(F32), (0, HEAD_DIM - N_HEADS_B)).reshape(1, HEAD_DIM)
    w_in_t = w_in.T
    w_gates = _gate_weights(w_in_t, g_lo)
    ones = jnp.ones((HEAD_DIM,), F32)
    gain = jnp.concatenate([
        jnp.tile(q_norm_a.astype(F32) * (scale * LOG2E), N_HEADS_A), jnp.tile(k_norm_a.astype(F32), N_HEADS_A), jnp.tile(ones, N_HEADS_A),
        jnp.tile(q_norm_b.astype(F32) * (scale * LOG2E), N_HEADS_B), jnp.tile(k_norm_b.astype(F32), N_HEADS_B), jnp.tile(ones, N_HEADS_B),
    ]).reshape(1, QKV_COLS)
    flag = jnp.concatenate([
        jnp.ones((2 * WIDTH_A,), F32), jnp.zeros((WIDTH_A,), F32), jnp.ones((2 * WIDTH_B,), F32), jnp.zeros((WIDTH_B,), F32),
    ]).reshape(1, QKV_COLS)

    xn = _rmsnorm_bf16(h, norm1_gain)
    qkv = _qkv_proj(xn, w_in_t, gain, flag)
    c, ct = _forget_cumsum(xn, w_in_t, b_f)
    y_a = _dilated_mixture(qkv)
    y_b = _forgetting_attention(qkv, c, ct)
    merged = _gated_up(xn, y_a, y_b, w_gates, w_up_a.astype(BF16), w_up_b.astype(BF16))
    h = _out_proj(merged, w_out.astype(BF16), h)

    hn = _rmsnorm_bf16(h, norm2_gain)
    wq_t = w_peer_q.T.astype(BF16)
    sk = peer_subkeys.reshape(PEER_HEADS * 2 * N_KEYS, PEER_QDIM // 2).astype(BF16)
    sc_t = _peer_scores(hn, wq_t, sk)
    e1, w2, tau = _peer_select(sc_t)
    return _peer_dense(hn, peer_u.astype(BF16), peer_v.T.astype(BF16), sc_t, e1, w2, tau, h)


def kernel(x, norm1_gain, w_in, b_forget, q_norm_a, k_norm_a, q_norm_b, k_norm_b,
           w_up_a, w_up_b, w_out, norm2_gain, w_peer_q, peer_subkeys, peer_u, peer_v):
    b, s, d = x.shape
    assert b == 1 and d == D_MODEL and s % (DILATION_GROUPS[-1][0]) == 0
    h = x.reshape(s, d)
    for layer in range(norm1_gain.shape[0]):
        h = _layer(h, norm1_gain[layer], w_in[layer], b_forget[layer], q_norm_a[layer], k_norm_a[layer],
                   q_norm_b[layer], k_norm_b[layer], w_up_a[layer], w_up_b[layer], w_out[layer], norm2_gain[layer],
                   w_peer_q[layer], peer_subkeys[layer], peer_u[layer], peer_v[layer])
    return h.reshape(b, s, d)
```

```python
import functools

import numpy as np
import jax
import jax.numpy as jnp
from jax import lax
from jax.experimental import pallas as pl
from jax.experimental.pallas import tpu as pltpu

F32 = jnp.float32
BF16 = jnp.bfloat16

D_MODEL = 4096
HEAD_DIM = 128
DILATION_GROUPS = ((128, 1), (512, 4), (2048, 16))
A_SLOTS = 6
N_HEADS_A = A_SLOTS * len(DILATION_GROUPS)
N_HEADS_B = D_MODEL // HEAD_DIM - N_HEADS_A
WIDTH_A = N_HEADS_A * HEAD_DIM
WIDTH_B = N_HEADS_B * HEAD_DIM
WIDTH_A_OUT = A_SLOTS * HEAD_DIM
QKV_COLS = 3 * WIDTH_A + 3 * WIDTH_B
ALIBI_MAX_EXP = 8.0
PEER_HEADS = 8
N_KEYS = 128
N_EXPERTS = N_KEYS * N_KEYS
PEER_TOPK = 16
PEER_QDIM = 256
EPS = 1e-6
NEG = -1e30
LOG2E = 1.4426950408889634

VMEM_LIMIT_BYTES = 56 * 1024 * 1024

NORM_TM = 512
QKV_TM, QKV_TN = 1024, 512
FORGET_TM = 256
FOX_TQ = 512
FOX_ROWS = 16
DIL_T = 256
DIL_ROWS = 8
GATE_TN, GATE_PAD = 512, 16
UP_TM, UP_TN = 512, 512
OUT_TM, OUT_TN = 1024, 512
PQ_TM = 512
TOPK_TL = 256
PEER_TM, PEER_TE = 512, 512
PEER_LC = 128
PACK = 16


def _cparams(sem, big=False):
    return pltpu.CompilerParams(dimension_semantics=sem, vmem_limit_bytes=VMEM_LIMIT_BYTES if big else None)


def _dot(a, b):
    return jnp.dot(a, b, preferred_element_type=F32)


def _dot_nt(a, b):
    return lax.dot_general(a, b, (((1,), (1,)), ((), ())), preferred_element_type=F32)


def _rmsnorm_kernel(x_ref, g_ref, o_ref):
    x = x_ref[...]
    ms = jnp.mean(x * x, axis=-1, keepdims=True)
    o_ref[...] = (x * lax.rsqrt(ms + EPS) * g_ref[...]).astype(o_ref.dtype)


def _rmsnorm_bf16(x2d, gain):
    s, d = x2d.shape
    return pl.pallas_call(
        _rmsnorm_kernel,
        grid=(s // NORM_TM,),
        in_specs=[pl.BlockSpec((NORM_TM, d), lambda i: (i, 0)), pl.BlockSpec((1, d), lambda i: (0, 0))],
        out_specs=pl.BlockSpec((NORM_TM, d), lambda i: (i, 0)),
        out_shape=jax.ShapeDtypeStruct((s, d), BF16),
        compiler_params=_cparams(("parallel",)),
        name="rmsnorm",
    )(x2d, gain.reshape(1, d))


def _qkv_kernel(a_ref, w_ref, gain_ref, flag_ref, o_ref, wb_ref):
    @pl.when(pl.program_id(1) == 0)
    def _():
        wb_ref[...] = w_ref[...].astype(BF16)

    acc = _dot_nt(a_ref[...], wb_ref[...])
    for c in range(QKV_TN // HEAD_DIM):
        sl = slice(c * HEAD_DIM, (c + 1) * HEAD_DIM)
        y = acc[:, sl]
        ms = jnp.mean(y * y, axis=-1, keepdims=True)
        yn = y * lax.rsqrt(ms + EPS) * gain_ref[:, sl]
        o_ref[:, sl] = jnp.where(flag_ref[:, sl] > 0.0, yn, y).astype(o_ref.dtype)


def _qkv_proj(xn, w_in_t, gain, flag):
    s, d = xn.shape
    n = gain.shape[1]
    return pl.pallas_call(
        _qkv_kernel,
        grid=(n // QKV_TN, s // QKV_TM),
        in_specs=[
            pl.BlockSpec((QKV_TM, d), lambda j, i: (i, 0)),
            pl.BlockSpec((QKV_TN, d), lambda j, i: (j, 0)),
            pl.BlockSpec((1, QKV_TN), lambda j, i: (0, j)),
            pl.BlockSpec((1, QKV_TN), lambda j, i: (0, j)),
        ],
        out_specs=pl.BlockSpec((QKV_TM, QKV_TN), lambda j, i: (i, j)),
        out_shape=jax.ShapeDtypeStruct((s, n), BF16),
        scratch_shapes=[pltpu.VMEM((QKV_TN, d), BF16)],
        compiler_params=_cparams(("parallel", "arbitrary"), big=True),
        name="qkv_proj",
    )(xn, w_in_t, gain, flag)


def _forget_kernel(a_ref, w_ref, b_ref, c_ref, ct_ref, carry_ref):
    @pl.when(pl.program_id(0) == 0)
    def _():
        carry_ref[...] = jnp.zeros_like(carry_ref)

    f = _dot_nt(a_ref[...], w_ref[...].astype(BF16)) + b_ref[...]
    logf = jnp.minimum(f, 0.0) - jnp.log1p(jnp.exp(-jnp.abs(f)))
    r = lax.broadcasted_iota(jnp.int32, (FORGET_TM, FORGET_TM), 0)
    c = lax.broadcasted_iota(jnp.int32, (FORGET_TM, FORGET_TM), 1)
    tri = (c <= r).astype(F32)
    cs = jnp.dot(tri, logf, precision=lax.Precision.HIGHEST, preferred_element_type=F32) + carry_ref[...]
    cs2 = cs * LOG2E
    c_ref[...] = cs2
    ct_ref[...] = cs2.T
    carry_ref[...] = cs[FORGET_TM - 1:FORGET_TM, :]


def _forget_cumsum(xn, w_in_t, b_f):
    s, d = xn.shape
    assert QKV_COLS % HEAD_DIM == 0
    return pl.pallas_call(
        _forget_kernel,
        grid=(s // FORGET_TM,),
        in_specs=[
            pl.BlockSpec((FORGET_TM, d), lambda i: (i, 0)),
            pl.BlockSpec((HEAD_DIM, d), lambda i: (QKV_COLS // HEAD_DIM, 0)),
            pl.BlockSpec((1, HEAD_DIM), lambda i: (0, 0)),
        ],
        out_specs=[pl.BlockSpec((FORGET_TM, HEAD_DIM), lambda i: (i, 0)), pl.BlockSpec((HEAD_DIM, FORGET_TM), lambda i: (0, i))],
        out_shape=[jax.ShapeDtypeStruct((s, HEAD_DIM), F32), jax.ShapeDtypeStruct((HEAD_DIM, s), F32)],
        scratch_shapes=[pltpu.VMEM((1, HEAD_DIM), F32)],
        compiler_params=_cparams(("arbitrary",)),
        name="forget_cumsum",
    )(xn, w_in_t, b_f)


def _alibi_slopes():
    n = N_HEADS_A
    return np.exp2(-np.float32(ALIBI_MAX_EXP) * np.arange(1, n + 1, dtype=np.float32) / np.float32(n)).astype(np.float32)


def _dilated_tiles():
    tiles, first = [], []
    for g, (window, dilation) in enumerate(DILATION_GROUPS):
        first.append(len(tiles))
        for back in range((window + DIL_T - 1) // DIL_T + 1):
            tiles.append((g, dilation, window, back))
    return tiles, first


def _dilated_kernel(slope_ref, q0_ref, q1_ref, q2_ref, k0_ref, k1_ref, k2_ref, v0_ref, v1_ref, v2_ref, y_ref,
                    bm_ref, s0_ref, s1_ref, s2_ref, p0_ref, p1_ref, p2_ref, m_ref, l_ref, acc_ref, *, seq):
    i = pl.program_id(1)
    T = DIL_T
    R = DIL_ROWS
    tiles, first = _dilated_tiles()
    none_tile = len(tiles)
    q_refs = (q0_ref, q1_ref, q2_ref)
    k_refs = (k0_ref, k1_ref, k2_ref)
    v_refs = (v0_ref, v1_ref, v2_ref)
    s_refs = (s0_ref, s1_ref, s2_ref)
    p_refs = (p0_ref, p1_ref, p2_ref)

    @pl.when(i == 0)
    def _():
        base = lax.broadcasted_iota(jnp.int32, (T, T), 0) - lax.broadcasted_iota(jnp.int32, (T, T), 1)
        for idx, (g, dilation, window, back) in enumerate(tiles):
            rel = base + T * back
            ok = jnp.where((base & (dilation - 1)) == 0, rel, -1)
            ok = jnp.where(ok <= window, ok, -1)
            bm_ref[idx] = jnp.where(ok >= 0, -slope_ref[g:g + 1, :] * rel.astype(F32), NEG)
        bm_ref[none_tile] = jnp.full((T, T), NEG, F32)

    for g, (window, dilation) in enumerate(DILATION_GROUPS):
        n_back = (window + T - 1) // T
        n_cols = min(n_back + 1, seq // T)
        first_blk = jnp.clip(i - n_back, 0, seq // T - n_cols)
        strip = pl.ds(pl.multiple_of(first_blk * T, T), n_cols * T)
        s_ref, p_ref = s_refs[g], p_refs[g]
        s_ref[...] = _dot_nt(q_refs[g][...], k_refs[g][strip, :])
        tile_of = []
        for c in range(n_cols):
            back = i - (first_blk + c)
            tile_of.append(jnp.where((back >= 0) & (back <= n_back), first[g] + back, none_tile))
        lane_blocks = [(c, h) for c in range(n_cols) for h in range(T // HEAD_DIM)]
        for rc in range(T // R):
            rows = slice(rc * R, (rc + 1) * R)
            ts = [s_ref[rows, c * T + h * HEAD_DIM:c * T + (h + 1) * HEAD_DIM]
                  + bm_ref[tile_of[c], rows, h * HEAD_DIM:(h + 1) * HEAD_DIM] for c, h in lane_blocks]
            m = jnp.max(functools.reduce(jnp.maximum, ts), axis=-1, keepdims=True)
            ps = [jnp.exp2(t - m) for t in ts]
            l = jnp.sum(functools.reduce(jnp.add, ps), axis=-1, keepdims=True)
            m_ref[g, rows, :] = jnp.broadcast_to(m, (R, HEAD_DIM))
            l_ref[g, rows, :] = jnp.broadcast_to(l, (R, HEAD_DIM))
            for (c, h), p in zip(lane_blocks, ps):
                p_ref[rows, c * T + h * HEAD_DIM:c * T + (h + 1) * HEAD_DIM] = p.astype(BF16)
        acc_ref[g] = _dot(p_ref[...], v_refs[g][strip, :])

    ms = [m_ref[g] for g in range(len(DILATION_GROUPS))]
    m_star = functools.reduce(jnp.maximum, ms)
    num = 0.0
    den = 0.0
    for g, m in enumerate(ms):
        w = jnp.exp2(m - m_star)
        num = num + w * acc_ref[g]
        den = den + w * l_ref[g]
    y_ref[...] = (num / den).astype(y_ref.dtype)


def _dilated_mixture(qkv):
    s = qkv.shape[0]
    T = DIL_T
    n_groups = len(DILATION_GROUPS)
    tiles, _ = _dilated_tiles()
    cols = [min((window + T - 1) // T + 1, s // T) for window, _ in DILATION_GROUPS]
    slopes = (_alibi_slopes() * np.float32(LOG2E)).reshape(n_groups, A_SLOTS).T
    slopes = jnp.asarray(np.broadcast_to(slopes[:, :, None], (A_SLOTS, n_groups, T)).copy())
    k0 = N_HEADS_A
    v0 = 2 * N_HEADS_A
    q_specs = [pl.BlockSpec((T, HEAD_DIM), lambda j, i, g=g: (i, g * A_SLOTS + j)) for g in range(n_groups)]
    k_specs = [pl.BlockSpec((s, HEAD_DIM), lambda j, i, g=g: (0, k0 + g * A_SLOTS + j)) for g in range(n_groups)]
    v_specs = [pl.BlockSpec((s, HEAD_DIM), lambda j, i, g=g: (0, v0 + g * A_SLOTS + j)) for g in range(n_groups)]
    return pl.pallas_call(
        functools.partial(_dilated_kernel, seq=s),
        grid=(A_SLOTS, s // T),
        in_specs=[pl.BlockSpec((None, n_groups, T), lambda j, i: (j, 0, 0))] + q_specs + k_specs + v_specs,
        out_specs=pl.BlockSpec((T, HEAD_DIM), lambda j, i: (i, j)),
        out_shape=jax.ShapeDtypeStruct((s, WIDTH_A_OUT), BF16),
        scratch_shapes=[
            pltpu.VMEM((len(tiles) + 1, T, T), F32),
            *[pltpu.VMEM((T, c * T), F32) for c in cols], *[pltpu.VMEM((T, c * T), BF16) for c in cols],
            pltpu.VMEM((n_groups, T, HEAD_DIM), F32), pltpu.VMEM((n_groups, T, HEAD_DIM), F32),
            pltpu.VMEM((n_groups, T, HEAD_DIM), F32),
        ],
        compiler_params=_cparams(("parallel", "arbitrary"), big=True),
        name="dilated_mixture",
    )(slopes, *([qkv] * (3 * n_groups)))


def _fox_kernel(q_ref, k_ref, v_ref, cq_ref, ck_ref, o_ref, s0_ref, s1_ref, p0_ref, p1_ref, cqr_ref, m_ref, l_ref,
                alpha_ref, acc_ref):
    h = pl.program_id(0)
    qb = pl.program_id(1)
    T = FOX_TQ
    R = FOX_ROWS
    G = T // HEAD_DIM
    s_refs = (s0_ref, s1_ref)
    p_refs = (p0_ref, p1_ref)
    lane = lax.broadcasted_iota(jnp.int32, (T, HEAD_DIM), 1)
    cq = jnp.sum(jnp.where(lane == h, cq_ref[...], 0.0), axis=-1, keepdims=True)
    cqr_ref[...] = jnp.broadcast_to(cq, (T, HEAD_DIM))
    m_ref[...] = jnp.full((T, HEAD_DIM), NEG, F32)
    l_ref[...] = jnp.zeros((T, HEAD_DIM), F32)
    acc_ref[...] = jnp.zeros((T, HEAD_DIM), F32)
    p1_ref[...] = jnp.zeros((T, T), BF16)

    def keys(j):
        return pl.ds(pl.multiple_of(j * T, T), T)

    def logits(j, slot):
        s_refs[slot][...] = _dot_nt(q_ref[...], k_ref[keys(j), :])

    def softmax(j, slot, diagonal):
        s_ref, p_ref = s_refs[slot], p_refs[slot]
        start = pl.multiple_of(j * T, T)
        for rc in range(T // R):
            rows = slice(rc * R, (rc + 1) * R)
            cq_r = cqr_ref[rows, :]
            sg = []
            for g in range(G):
                lo = g * HEAD_DIM
                if diagonal and lo > rc * R + R - 1:
                    sg.append(None)
                    continue
                t = s_ref[rows, lo:lo + HEAD_DIM] + cq_r - ck_ref[:, pl.ds(start + lo, HEAD_DIM)]
                if diagonal and lo + HEAD_DIM - 1 > rc * R:
                    rel = (lax.broadcasted_iota(jnp.int32, (R, HEAD_DIM), 0) - lax.broadcasted_iota(jnp.int32, (R, HEAD_DIM), 1))
                    t = jnp.where(rel >= lo - rc * R, t, NEG)
                sg.append(t)
            live = [t for t in sg if t is not None]
            m_old = m_ref[rows, :]
            m_new = jnp.maximum(m_old, jnp.max(functools.reduce(jnp.maximum, live), axis=-1, keepdims=True))
            alpha = jnp.exp2(m_old - m_new)
            ps = [None if t is None else jnp.exp2(t - m_new) for t in sg]
            row_sum = jnp.sum(functools.reduce(jnp.add, [p for p in ps if p is not None]), axis=-1, keepdims=True)
            l_ref[rows, :] = alpha * l_ref[rows, :] + row_sum
            m_ref[rows, :] = m_new
            alpha_ref[rows, :] = alpha
            for g in range(G):
                lo = g * HEAD_DIM
                p_ref[rows, lo:lo + HEAD_DIM] = (jnp.zeros((R, HEAD_DIM), BF16) if ps[g] is None else ps[g].astype(BF16))

    def stage(j, slot, diagonal):
        other = 1 - slot
        if not diagonal:
            logits(j + 1, other)
        pv = _dot(p_refs[other][...], v_ref[keys(jnp.maximum(j - 1, 0)), :])
        softmax(j, slot, diagonal)
        acc_ref[...] = (acc_ref[...] + pv) * alpha_ref[...]

    logits(0, 0)

    def pair(i, carry):
        stage(2 * i, 0, False)
        stage(2 * i + 1, 1, False)
        return carry

    lax.fori_loop(0, qb // 2, pair, 0)

    def finish(slot):
        stage(qb, slot, True)
        acc = acc_ref[...] + _dot(p_refs[slot][...], v_ref[keys(qb), :])
        o_ref[...] = (acc / l_ref[...]).astype(o_ref.dtype)

    @pl.when(qb % 2 == 0)
    def _():
        finish(0)

    @pl.when(qb % 2 == 1)
    def _():
        stage(qb - 1, 0, False)
        finish(1)


def _forgetting_attention(qkv, c, ct):
    s = qkv.shape[0]
    T = FOX_TQ
    q0 = 3 * WIDTH_A // HEAD_DIM
    k0 = q0 + N_HEADS_B
    v0 = k0 + N_HEADS_B
    ck = ct.reshape(HEAD_DIM, 1, s)
    return pl.pallas_call(
        _fox_kernel,
        grid=(N_HEADS_B, s // T),
        in_specs=[
            pl.BlockSpec((T, HEAD_DIM), lambda h, i: (i, q0 + h)),
            pl.BlockSpec((s, HEAD_DIM), lambda h, i: (0, k0 + h)),
            pl.BlockSpec((s, HEAD_DIM), lambda h, i: (0, v0 + h)),
            pl.BlockSpec((T, HEAD_DIM), lambda h, i: (i, 0)),
            pl.BlockSpec((None, 1, s), lambda h, i: (h, 0, 0)),
        ],
        out_specs=pl.BlockSpec((T, HEAD_DIM), lambda h, i: (i, h)),
        out_shape=jax.ShapeDtypeStruct((s, WIDTH_B), BF16),
        scratch_shapes=[
            pltpu.VMEM((T, T), F32), pltpu.VMEM((T, T), F32), pltpu.VMEM((T, T), BF16), pltpu.VMEM((T, T), BF16),
            pltpu.VMEM((T, HEAD_DIM), F32), pltpu.VMEM((T, HEAD_DIM), F32), pltpu.VMEM((T, HEAD_DIM), F32),
            pltpu.VMEM((T, HEAD_DIM), F32), pltpu.VMEM((T, HEAD_DIM), F32),
        ],
        compiler_params=_cparams(("parallel", "arbitrary")),
        name="fox_attention",
    )(qkv, qkv, qkv, c, ck)


def _gate_cast_kernel(main_ref, extra_ref, o_ref, *, row_off):
    x = jnp.concatenate([main_ref[...], extra_ref[...]], axis=0)
    o_ref[...] = x[row_off:row_off + GATE_TN, :].astype(o_ref.dtype)


def _gate_weights(w_in_t, g_lo):
    d = w_in_t.shape[1]
    blk0, row_off = divmod(g_lo, GATE_TN)
    assert row_off <= GATE_PAD and GATE_TN % GATE_PAD == 0 and g_lo + 2 * d <= w_in_t.shape[0]
    return pl.pallas_call(
        functools.partial(_gate_cast_kernel, row_off=row_off),
        grid=(2 * d // GATE_TN,),
        in_specs=[
            pl.BlockSpec((GATE_TN, d), lambda j: (blk0 + j, 0)),
            pl.BlockSpec((GATE_PAD, d), lambda j: ((blk0 + j + 1) * (GATE_TN // GATE_PAD), 0)),
        ],
        out_specs=pl.BlockSpec((GATE_TN, d), lambda j: (j, 0)),
        out_shape=jax.ShapeDtypeStruct((2 * d, d), BF16),
        compiler_params=_cparams(("parallel",), big=True),
        name="gate_weights",
    )(w_in_t, w_in_t)


def _up_kernel(xn_ref, ya_ref, yb_ref, wga_ref, wgb_ref, wua_ref, wub_ref, v_ref, o_ref, vt_ref):
    vt_ref[...] = v_ref[...].T.astype(vt_ref.dtype)
    xn = xn_ref[...]
    ga = jax.nn.sigmoid(_dot_nt(xn, wga_ref[...]))
    ua = _dot(ya_ref[...], wua_ref[...])
    part = ga * ua
    gb = jax.nn.sigmoid(_dot_nt(xn, wgb_ref[...]))
    ub = _dot(yb_ref[...], wub_ref[...])
    o_ref[...] = (part + gb * ub).astype(o_ref.dtype)


def _gated_up(xn, ya, yb, w_gates, w_up_a, w_up_b, peer_v):
    s, d = xn.shape
    n_i, n_j = s // UP_TM, d // UP_TN
    v_rows = peer_v.shape[0] // (n_i * n_j)
    assert v_rows * n_i * n_j == peer_v.shape[0] and v_rows % HEAD_DIM == 0
    row = lambda width: pl.BlockSpec((UP_TM, width), lambda i, j: (i, 0))
    col = lambda depth: pl.BlockSpec((depth, UP_TN), lambda i, j: (0, j))
    gate_a = pl.BlockSpec((UP_TN, d), lambda i, j: (j, 0))
    gate_b = pl.BlockSpec((UP_TN, d), lambda i, j: (j + d // UP_TN, 0))
    return pl.pallas_call(
        _up_kernel,
        grid=(s // UP_TM, d // UP_TN),
        in_specs=[row(d), row(WIDTH_A_OUT), row(WIDTH_B), gate_a, gate_b, col(WIDTH_A_OUT), col(WIDTH_B),
                  pl.BlockSpec((v_rows, d), lambda i, j: (i * n_j + j, 0))],
        out_specs=[pl.BlockSpec((UP_TM, UP_TN), lambda i, j: (i, j)),
                   pl.BlockSpec((d, v_rows), lambda i, j: (0, i * n_j + j))],
        out_shape=[jax.ShapeDtypeStruct((s, d), BF16), jax.ShapeDtypeStruct((d, peer_v.shape[0]), BF16)],
        compiler_params=_cparams(("parallel", "arbitrary"), big=True),
        name="gated_up",
    )(xn, ya, yb, w_gates, w_gates, w_up_a, w_up_b, peer_v)


def _out_kernel(a_ref, w_ref, x_ref, u_ref, o_ref, ub_ref):
    ub_ref[...] = u_ref[...].astype(ub_ref.dtype)
    o_ref[...] = x_ref[...] + _dot(a_ref[...], w_ref[...])


def _out_proj(merged, w_out, x2d, peer_u):
    s, d = x2d.shape
    n_i, n_j = s // OUT_TM, d // OUT_TN
    u_rows = peer_u.shape[0] // (n_i * n_j)
    assert u_rows * n_i * n_j == peer_u.shape[0] and u_rows % PACK == 0
    return pl.pallas_call(
        _out_kernel,
        grid=(n_i, n_j),
        in_specs=[
            pl.BlockSpec((OUT_TM, d), lambda i, j: (i, 0)),
            pl.BlockSpec((d, OUT_TN), lambda i, j: (0, j)),
            pl.BlockSpec((OUT_TM, OUT_TN), lambda i, j: (i, j)),
            pl.BlockSpec((u_rows, d), lambda i, j: (i * n_j + j, 0)),
        ],
        out_specs=[pl.BlockSpec((OUT_TM, OUT_TN), lambda i, j: (i, j)),
                   pl.BlockSpec((u_rows, d), lambda i, j: (i * n_j + j, 0))],
        out_shape=[jax.ShapeDtypeStruct((s, d), F32), jax.ShapeDtypeStruct(peer_u.shape, BF16)],
        compiler_params=_cparams(("parallel", "arbitrary"), big=True),
        name="out_proj",
    )(merged, w_out, x2d, peer_u)


def _peer_scores_kernel(wq_ref, hn_ref, sk_ref, o_ref):
    qt = _dot_nt(wq_ref[...], hn_ref[...]).astype(BF16)
    for r in range(PEER_HEADS * 2):
        rows = slice(r * N_KEYS, (r + 1) * N_KEYS)
        o_ref[rows, :] = _dot(sk_ref[rows, :], qt[rows, :])


def _peer_scores(hn, wq_t, sk):
    s, d = hn.shape
    rows = PEER_HEADS * 2 * N_KEYS
    assert PEER_QDIM // 2 == N_KEYS
    once = pl.Buffered(1)
    return pl.pallas_call(
        _peer_scores_kernel,
        grid=(s // PQ_TM,),
        in_specs=[
            pl.BlockSpec((PEER_HEADS * PEER_QDIM, d), lambda i: (0, 0), pipeline_mode=once),
            pl.BlockSpec((PQ_TM, d), lambda i: (i, 0)),
            pl.BlockSpec((rows, PEER_QDIM // 2), lambda i: (0, 0), pipeline_mode=once),
        ],
        out_specs=pl.BlockSpec((rows, PQ_TM), lambda i: (0, i)),
        out_shape=jax.ShapeDtypeStruct((rows, s), F32),
        compiler_params=_cparams(("parallel",), big=True),
        name="peer_scores",
    )(wq_t, hn, sk)


def _stair_width(a):
    return PEER_TOPK // (a + 1)


def _topk_kernel(sc_ref, e1_ref, w2_ref, tau_ref):
    K = PEER_TOPK
    s1 = sc_ref[0:N_KEYS, :]
    s2 = sc_ref[N_KEYS:2 * N_KEYS, :]
    ninf = -jnp.inf

    def top(cur, count):
        outs = []
        for _ in range(count):
            mk = jnp.max(cur, axis=0, keepdims=True)
            outs.append(mk)
            cur = jnp.where(cur == mk, ninf, cur)
        return outs

    t1 = top(s1, K)
    t2 = top(s2, K)
    t2_all = jnp.concatenate(t2, axis=0)
    rank = lax.broadcasted_iota(jnp.int32, t2_all.shape, 0)
    n_wide = K // 2
    blocks = [jnp.where(rank < _stair_width(a), t1[a] + t2_all, ninf) for a in range(n_wide)]
    blocks.append(jnp.concatenate(t1[n_wide:], axis=0) + t2[0])
    cand = jnp.concatenate(blocks, axis=0)
    thr = top(cand, K)[K - 1]
    z = jnp.sum(jnp.where(cand >= thr, jnp.exp(cand - (t1[0] + t2[0])), 0.0), axis=0, keepdims=True)
    e1_ref[...] = jnp.exp(s1 - t1[0])
    w2_ref[...] = jnp.exp(s2 - t2[0]) / z
    tau = jnp.full(s1.shape, jnp.inf, F32)
    for b in range(K):
        tau = jnp.where(s1 + t2[b] >= thr, t2[b], tau)
    tau_ref[...] = tau


def _peer_select(sc_t):
    s = sc_t.shape[1]
    rows = PEER_HEADS * N_KEYS
    spec = pl.BlockSpec((N_KEYS, TOPK_TL), lambda i, j: (j, i))
    sds = jax.ShapeDtypeStruct((rows, s), F32)
    return pl.pallas_call(
        _topk_kernel,
        grid=(s // TOPK_TL, PEER_HEADS),
        in_specs=[pl.BlockSpec((2 * N_KEYS, TOPK_TL), lambda i, j: (j, i))],
        out_specs=[spec, spec, spec],
        out_shape=[sds, sds, sds],
        compiler_params=_cparams(("parallel", "arbitrary")),
        name="peer_select",
    )(sc_t)


def _peer_kernel(hn_ref, u_ref, vt_ref, s2_ref, tau_ref, e1_ref, w2_ref, h_ref, o_ref, acc_ref, a0_ref, a1_ref,
                 ag0_ref, ag1_ref):
    ej = pl.program_id(1)
    HT = PEER_TM // 2
    keys_per_tile = PEER_TE // N_KEYS
    lane_chunks = HT // PEER_LC
    n_chunks = keys_per_tile * lane_chunks
    d = u_ref.shape[1]
    kc_size = d // n_chunks
    row_size = d // n_chunks

    @pl.when(ej == 0)
    def _():
        acc_ref[...] = jnp.zeros_like(acc_ref)

    inv_sqrt2 = 0.7071067811865476

    def gate_chunk(a_ref, ag_ref, half, c):
        b, lc = divmod(c, lane_chunks)
        i1 = ej * keys_per_tile + b
        rows = slice(b * N_KEYS, (b + 1) * N_KEYS)
        cols = slice(lc * PEER_LC, (lc + 1) * PEER_LC)
        tok = slice(half * HT + lc * PEER_LC, half * HT + (lc + 1) * PEER_LC)
        g = jnp.zeros((N_KEYS, PEER_LC), F32)
        for h in range(PEER_HEADS):
            tau = tau_ref[i1, h:h + 1, tok]
            e1 = e1_ref[i1, h:h + 1, tok]
            g = g + jnp.where(s2_ref[h, :, tok] >= tau, w2_ref[h * N_KEYS:(h + 1) * N_KEYS, tok], 0.0) * e1
        a = a_ref[rows, cols]
        act = 0.5 * a * (1.0 + lax.erf(a * inv_sqrt2))
        ag_ref[rows, cols] = (act * g).astype(BF16)

    a0_ref[...] = _dot_nt(u_ref[...], hn_ref[0:HT, :])
    for c in range(n_chunks):
        kc = slice(c * kc_size, (c + 1) * kc_size)
        piece = _dot_nt(u_ref[:, kc], hn_ref[HT:PEER_TM, kc])
        if c == 0:
            a1_ref[...] = piece
        else:
            a1_ref[...] += piece
        gate_chunk(a0_ref, ag0_ref, 0, c)
    for c in range(n_chunks):
        rows = slice(c * row_size, (c + 1) * row_size)
        acc_ref[rows, 0:HT] += _dot(vt_ref[rows, :], ag0_ref[...])
        gate_chunk(a1_ref, ag1_ref, 1, c)
    acc_ref[:, HT:PEER_TM] += _dot(vt_ref[...], ag1_ref[...])

    @pl.when(ej == pl.num_programs(1) - 1)
    def _():
        for c in range(d // PEER_TM):
            cols = slice(c * PEER_TM, (c + 1) * PEER_TM)
            o_ref[:, cols] = h_ref[:, cols] + acc_ref[cols, :].T


def _peer_dense(hn, u, vt, sc_t, e1, w2, tau, h):
    s, d = hn.shape
    once = pl.Buffered(1)
    tau_r = tau.reshape(PEER_HEADS, N_KEYS, s).transpose(1, 0, 2)
    e1_r = e1.reshape(PEER_HEADS, N_KEYS, s).transpose(1, 0, 2)
    by_key = pl.BlockSpec((N_KEYS, PEER_HEADS, PEER_TM), lambda i, j: (0, 0, i), pipeline_mode=once)
    sc4 = sc_t.reshape(PEER_HEADS, 2, N_KEYS, s)
    half = PEER_TM // 2
    return pl.pallas_call(
        _peer_kernel,
        grid=(s // PEER_TM, N_EXPERTS // PEER_TE),
        in_specs=[
            pl.BlockSpec((PEER_TM, d), lambda i, j: (i, 0), pipeline_mode=once),
            pl.BlockSpec((PEER_TE, d), lambda i, j: (j, 0)),
            pl.BlockSpec((d, PEER_TE), lambda i, j: (0, j)),
            pl.BlockSpec((PEER_HEADS, None, N_KEYS, PEER_TM), lambda i, j: (0, 1, 0, i), pipeline_mode=once),
            by_key,
            by_key,
            pl.BlockSpec((PEER_HEADS * N_KEYS, PEER_TM), lambda i, j: (0, i), pipeline_mode=once),
            pl.BlockSpec((PEER_TM, d), lambda i, j: (i, 0), pipeline_mode=once),
        ],
        out_specs=pl.BlockSpec((PEER_TM, d), lambda i, j: (i, 0), pipeline_mode=once),
        out_shape=jax.ShapeDtypeStruct((s, d), F32),
        scratch_shapes=[pltpu.VMEM((d, PEER_TM), F32),
                        pltpu.VMEM((PEER_TE, half), F32), pltpu.VMEM((PEER_TE, half), F32),
                        pltpu.VMEM((PEER_TE, half), BF16), pltpu.VMEM((PEER_TE, half), BF16)],
        compiler_params=_cparams(("parallel", "arbitrary"), big=True),
        name="peer_dense",
    )(hn, u, vt, sc4, tau_r, e1_r, w2, h)


def _layer(h, norm1_gain, w_in, b_forget, q_norm_a, k_norm_a, q_norm_b, k_norm_b,
           w_up_a, w_up_b, w_out, norm2_gain, w_peer_q, peer_subkeys, peer_u, peer_v):
    d = D_MODEL
    scale = HEAD_DIM ** -0.5
    f_lo = QKV_COLS
    g_lo = f_lo + N_HEADS_B
    b_f = jnp.pad(b_forget.astype(F32), (0, HEAD_DIM - N_HEADS_B)).reshape(1, HEAD_DIM)
    w_in_t = w_in.T
    w_gates = _gate_weights(w_in_t, g_lo)
    ones = jnp.ones((HEAD_DIM,), F32)
    gain = jnp.concatenate([
        jnp.tile(q_norm_a.astype(F32) * (scale * LOG2E), N_HEADS_A), jnp.tile(k_norm_a.astype(F32), N_HEADS_A), jnp.tile(ones, N_HEADS_A),
        jnp.tile(q_norm_b.astype(F32) * (scale * LOG2E), N_HEADS_B), jnp.tile(k_norm_b.astype(F32), N_HEADS_B), jnp.tile(ones, N_HEADS_B),
    ]).reshape(1, QKV_COLS)
    flag = jnp.concatenate([
        jnp.ones((2 * WIDTH_A,), F32), jnp.zeros((WIDTH_A,), F32), jnp.ones((2 * WIDTH_B,), F32), jnp.zeros((WIDTH_B,), F32),
    ]).reshape(1, QKV_COLS)

    xn = _rmsnorm_bf16(h, norm1_gain)
    qkv = _qkv_proj(xn, w_in_t, gain, flag)
    c, ct = _forget_cumsum(xn, w_in_t, b_f)
    y_a = _dilated_mixture(qkv)
    y_b = _forgetting_attention(qkv, c, ct)
    merged, v_t = _gated_up(xn, y_a, y_b, w_gates, w_up_a.astype(BF16), w_up_b.astype(BF16), peer_v)
    h, u_b = _out_proj(merged, w_out.astype(BF16), h, peer_u)

    hn = _rmsnorm_bf16(h, norm2_gain)
    wq_t = w_peer_q.T.astype(BF16)
    sk = peer_subkeys.reshape(PEER_HEADS * 2 * N_KEYS, PEER_QDIM // 2).astype(BF16)
    sc_t = _peer_scores(hn, wq_t, sk)
    e1, w2, tau = _peer_select(sc_t)
    return _peer_dense(hn, u_b, v_t, sc_t, e1, w2, tau, h)


def kernel(x, norm1_gain, w_in, b_forget, q_norm_a, k_norm_a, q_norm_b, k_norm_b,
           w_up_a, w_up_b, w_out, norm2_gain, w_peer_q, peer_subkeys, peer_u, peer_v):
    b, s, d = x.shape
    assert b == 1 and d == D_MODEL and s % (DILATION_GROUPS[-1][0]) == 0
    h = x.reshape(s, d)
    for layer in range(norm1_gain.shape[0]):
        h = _layer(h, norm1_gain[layer], w_in[layer], b_forget[layer], q_norm_a[layer], k_norm_a[layer],
                   q_norm_b[layer], k_norm_b[layer], w_up_a[layer], w_up_b[layer], w_out[layer], norm2_gain[layer],
                   w_peer_q[layer], peer_subkeys[layer], peer_u[layer], peer_v[layer])
    return h.reshape(b, s, d)
```

```python
import functools

import numpy as np
import jax
import jax.numpy as jnp
from jax import lax
from jax.experimental import pallas as pl
from jax.experimental.pallas import tpu as pltpu

F32 = jnp.float32
BF16 = jnp.bfloat16

D_MODEL = 4096
HEAD_DIM = 128
DILATION_GROUPS = ((128, 1), (512, 4), (2048, 16))
A_SLOTS = 6
N_HEADS_A = A_SLOTS * len(DILATION_GROUPS)
N_HEADS_B = D_MODEL // HEAD_DIM - N_HEADS_A
WIDTH_A = N_HEADS_A * HEAD_DIM
WIDTH_B = N_HEADS_B * HEAD_DIM
WIDTH_A_OUT = A_SLOTS * HEAD_DIM
QKV_COLS = 3 * WIDTH_A + 3 * WIDTH_B
ALIBI_MAX_EXP = 8.0
PEER_HEADS = 8
N_KEYS = 128
N_EXPERTS = N_KEYS * N_KEYS
PEER_TOPK = 16
PEER_QDIM = 256
EPS = 1e-6
NEG = -1e30
LOG2E = 1.4426950408889634

VMEM_LIMIT_BYTES = 56 * 1024 * 1024

NORM_TM = 512
QKV_TM, QKV_TN = 1024, 512
FORGET_TM = 256
FOX_TQ = 512
FOX_ROWS = 16
DIL_T = 256
DIL_ROWS = 8
GATE_TN, GATE_PAD = 512, 16
UP_TM, UP_TN = 512, 512
OUT_TM, OUT_TN = 1024, 512
PQ_TM = 512
TOPK_TL = 256
PEER_TM, PEER_TE = 512, 512
PEER_LC = 128
PACK = 16


def _cparams(sem, big=False):
    return pltpu.CompilerParams(dimension_semantics=sem, vmem_limit_bytes=VMEM_LIMIT_BYTES if big else None)


def _dot(a, b):
    return jnp.dot(a, b, preferred_element_type=F32)


def _dot_nt(a, b):
    return lax.dot_general(a, b, (((1,), (1,)), ((), ())), preferred_element_type=F32)


def _rmsnorm_kernel(x_ref, g_ref, o_ref):
    x = x_ref[...]
    ms = jnp.mean(x * x, axis=-1, keepdims=True)
    o_ref[...] = (x * lax.rsqrt(ms + EPS) * g_ref[...]).astype(o_ref.dtype)


def _rmsnorm_bf16(x2d, gain):
    s, d = x2d.shape
    return pl.pallas_call(
        _rmsnorm_kernel,
        grid=(s // NORM_TM,),
        in_specs=[pl.BlockSpec((NORM_TM, d), lambda i: (i, 0)), pl.BlockSpec((1, d), lambda i: (0, 0))],
        out_specs=pl.BlockSpec((NORM_TM, d), lambda i: (i, 0)),
        out_shape=jax.ShapeDtypeStruct((s, d), BF16),
        compiler_params=_cparams(("parallel",)),
        name="rmsnorm",
    )(x2d, gain.reshape(1, d))


def _qkv_kernel(a_ref, w_ref, gain_ref, flag_ref, o_ref, wb_ref):
    @pl.when(pl.program_id(1) == 0)
    def _():
        wb_ref[...] = w_ref[...].astype(BF16)

    acc = _dot_nt(a_ref[...], wb_ref[...])
    for c in range(QKV_TN // HEAD_DIM):
        sl = slice(c * HEAD_DIM, (c + 1) * HEAD_DIM)
        y = acc[:, sl]
        ms = jnp.mean(y * y, axis=-1, keepdims=True)
        yn = y * lax.rsqrt(ms + EPS) * gain_ref[:, sl]
        o_ref[:, sl] = jnp.where(flag_ref[:, sl] > 0.0, yn, y).astype(o_ref.dtype)


def _qkv_proj(xn, w_in_t, gain, flag):
    s, d = xn.shape
    n = gain.shape[1]
    return pl.pallas_call(
        _qkv_kernel,
        grid=(n // QKV_TN, s // QKV_TM),
        in_specs=[
            pl.BlockSpec((QKV_TM, d), lambda j, i: (i, 0)),
            pl.BlockSpec((QKV_TN, d), lambda j, i: (j, 0)),
            pl.BlockSpec((1, QKV_TN), lambda j, i: (0, j)),
            pl.BlockSpec((1, QKV_TN), lambda j, i: (0, j)),
        ],
        out_specs=pl.BlockSpec((QKV_TM, QKV_TN), lambda j, i: (i, j)),
        out_shape=jax.ShapeDtypeStruct((s, n), BF16),
        scratch_shapes=[pltpu.VMEM((QKV_TN, d), BF16)],
        compiler_params=_cparams(("parallel", "arbitrary"), big=True),
        name="qkv_proj",
    )(xn, w_in_t, gain, flag)


def _forget_kernel(a_ref, w_ref, b_ref, c_ref, ct_ref, carry_ref):
    @pl.when(pl.program_id(0) == 0)
    def _():
        carry_ref[...] = jnp.zeros_like(carry_ref)

    f = _dot_nt(a_ref[...], w_ref[...].astype(BF16)) + b_ref[...]
    logf = jnp.minimum(f, 0.0) - jnp.log1p(jnp.exp(-jnp.abs(f)))
    r = lax.broadcasted_iota(jnp.int32, (FORGET_TM, FORGET_TM), 0)
    c = lax.broadcasted_iota(jnp.int32, (FORGET_TM, FORGET_TM), 1)
    tri = (c <= r).astype(F32)
    cs = jnp.dot(tri, logf, precision=lax.Precision.HIGHEST, preferred_element_type=F32) + carry_ref[...]
    cs2 = cs * LOG2E
    c_ref[...] = cs2
    ct_ref[...] = cs2.T
    carry_ref[...] = cs[FORGET_TM - 1:FORGET_TM, :]


def _forget_cumsum(xn, w_in_t, b_f):
    s, d = xn.shape
    assert QKV_COLS % HEAD_DIM == 0
    return pl.pallas_call(
        _forget_kernel,
        grid=(s // FORGET_TM,),
        in_specs=[
            pl.BlockSpec((FORGET_TM, d), lambda i: (i, 0)),
            pl.BlockSpec((HEAD_DIM, d), lambda i: (QKV_COLS // HEAD_DIM, 0)),
            pl.BlockSpec((1, HEAD_DIM), lambda i: (0, 0)),
        ],
        out_specs=[pl.BlockSpec((FORGET_TM, HEAD_DIM), lambda i: (i, 0)), pl.BlockSpec((HEAD_DIM, FORGET_TM), lambda i: (0, i))],
        out_shape=[jax.ShapeDtypeStruct((s, HEAD_DIM), F32), jax.ShapeDtypeStruct((HEAD_DIM, s), F32)],
        scratch_shapes=[pltpu.VMEM((1, HEAD_DIM), F32)],
        compiler_params=_cparams(("arbitrary",)),
        name="forget_cumsum",
    )(xn, w_in_t, b_f)


def _alibi_slopes():
    n = N_HEADS_A
    return np.exp2(-np.float32(ALIBI_MAX_EXP) * np.arange(1, n + 1, dtype=np.float32) / np.float32(n)).astype(np.float32)


def _dilated_tiles():
    tiles, first = [], []
    for g, (window, dilation) in enumerate(DILATION_GROUPS):
        first.append(len(tiles))
        for back in range((window + DIL_T - 1) // DIL_T + 1):
            tiles.append((g, dilation, window, back))
    return tiles, first


def _dilated_kernel(slope_ref, q0_ref, q1_ref, q2_ref, k0_ref, k1_ref, k2_ref, v0_ref, v1_ref, v2_ref, y_ref,
                    bm_ref, s0_ref, s1_ref, s2_ref, p0_ref, p1_ref, p2_ref, m_ref, l_ref, acc_ref, *, seq):
    i = pl.program_id(1)
    T = DIL_T
    R = DIL_ROWS
    tiles, first = _dilated_tiles()
    none_tile = len(tiles)
    q_refs = (q0_ref, q1_ref, q2_ref)
    k_refs = (k0_ref, k1_ref, k2_ref)
    v_refs = (v0_ref, v1_ref, v2_ref)
    s_refs = (s0_ref, s1_ref, s2_ref)
    p_refs = (p0_ref, p1_ref, p2_ref)

    @pl.when(i == 0)
    def _():
        base = lax.broadcasted_iota(jnp.int32, (T, T), 0) - lax.broadcasted_iota(jnp.int32, (T, T), 1)
        for idx, (g, dilation, window, back) in enumerate(tiles):
            rel = base + T * back
            ok = jnp.where((base & (dilation - 1)) == 0, rel, -1)
            ok = jnp.where(ok <= window, ok, -1)
            bm_ref[idx] = jnp.where(ok >= 0, -slope_ref[g:g + 1, :] * rel.astype(F32), NEG)
        bm_ref[none_tile] = jnp.full((T, T), NEG, F32)

    geom = []
    for g, (window, dilation) in enumerate(DILATION_GROUPS):
        n_back = (window + T - 1) // T
        n_cols = min(n_back + 1, seq // T)
        first_blk = jnp.clip(i - n_back, 0, seq // T - n_cols)
        strip = pl.ds(pl.multiple_of(first_blk * T, T), n_cols * T)
        geom.append((n_back, n_cols, first_blk, strip))
    for g, (n_back, n_cols, first_blk, strip) in enumerate(geom):
        s_refs[g][...] = _dot_nt(q_refs[g][...], k_refs[g][strip, :])
    for g, (n_back, n_cols, first_blk, strip) in enumerate(geom):
        s_ref, p_ref = s_refs[g], p_refs[g]
        tile_of = []
        for c in range(n_cols):
            back = i - (first_blk + c)
            tile_of.append(jnp.where((back >= 0) & (back <= n_back), first[g] + back, none_tile))
        lane_blocks = [(c, h) for c in range(n_cols) for h in range(T // HEAD_DIM)]
        for rc in range(T // R):
            rows = slice(rc * R, (rc + 1) * R)
            ts = [s_ref[rows, c * T + h * HEAD_DIM:c * T + (h + 1) * HEAD_DIM]
                  + bm_ref[tile_of[c], rows, h * HEAD_DIM:(h + 1) * HEAD_DIM] for c, h in lane_blocks]
            m = jnp.max(functools.reduce(jnp.maximum, ts), axis=-1, keepdims=True)
            ps = [jnp.exp2(t - m) for t in ts]
            l = jnp.sum(functools.reduce(jnp.add, ps), axis=-1, keepdims=True)
            m_ref[g, rows, :] = jnp.broadcast_to(m, (R, HEAD_DIM))
            l_ref[g, rows, :] = jnp.broadcast_to(l, (R, HEAD_DIM))
            for (c, h), p in zip(lane_blocks, ps):
                p_ref[rows, c * T + h * HEAD_DIM:c * T + (h + 1) * HEAD_DIM] = p.astype(BF16)
        acc_ref[g] = _dot(p_ref[...], v_refs[g][strip, :])

    ms = [m_ref[g] for g in range(len(DILATION_GROUPS))]
    m_star = functools.reduce(jnp.maximum, ms)
    num = 0.0
    den = 0.0
    for g, m in enumerate(ms):
        w = jnp.exp2(m - m_star)
        num = num + w * acc_ref[g]
        den = den + w * l_ref[g]
    y_ref[...] = (num / den).astype(y_ref.dtype)


def _dilated_mixture(qkv):
    s = qkv.shape[0]
    T = DIL_T
    n_groups = len(DILATION_GROUPS)
    tiles, _ = _dilated_tiles()
    cols = [min((window + T - 1) // T + 1, s // T) for window, _ in DILATION_GROUPS]
    slopes = (_alibi_slopes() * np.float32(LOG2E)).reshape(n_groups, A_SLOTS).T
    slopes = jnp.asarray(np.broadcast_to(slopes[:, :, None], (A_SLOTS, n_groups, T)).copy())
    k0 = N_HEADS_A
    v0 = 2 * N_HEADS_A
    q_specs = [pl.BlockSpec((T, HEAD_DIM), lambda j, i, g=g: (i, g * A_SLOTS + j)) for g in range(n_groups)]
    k_specs = [pl.BlockSpec((s, HEAD_DIM), lambda j, i, g=g: (0, k0 + g * A_SLOTS + j)) for g in range(n_groups)]
    v_specs = [pl.BlockSpec((s, HEAD_DIM), lambda j, i, g=g: (0, v0 + g * A_SLOTS + j)) for g in range(n_groups)]
    return pl.pallas_call(
        functools.partial(_dilated_kernel, seq=s),
        grid=(A_SLOTS, s // T),
        in_specs=[pl.BlockSpec((None, n_groups, T), lambda j, i: (j, 0, 0))] + q_specs + k_specs + v_specs,
        out_specs=pl.BlockSpec((T, HEAD_DIM), lambda j, i: (i, j)),
        out_shape=jax.ShapeDtypeStruct((s, WIDTH_A_OUT), BF16),
        scratch_shapes=[
            pltpu.VMEM((len(tiles) + 1, T, T), F32),
            *[pltpu.VMEM((T, c * T), F32) for c in cols], *[pltpu.VMEM((T, c * T), BF16) for c in cols],
            pltpu.VMEM((n_groups, T, HEAD_DIM), F32), pltpu.VMEM((n_groups, T, HEAD_DIM), F32),
            pltpu.VMEM((n_groups, T, HEAD_DIM), F32),
        ],
        compiler_params=_cparams(("parallel", "arbitrary"), big=True),
        name="dilated_mixture",
    )(slopes, *([qkv] * (3 * n_groups)))


def _fox_kernel(q_ref, k_ref, v_ref, cq_ref, ck_ref, o_ref, s0_ref, s1_ref, p0_ref, p1_ref, cqr_ref, m_ref, l_ref,
                alpha_ref, acc_ref):
    h = pl.program_id(0)
    qb = pl.program_id(1)
    T = FOX_TQ
    R = FOX_ROWS
    G = T // HEAD_DIM
    s_refs = (s0_ref, s1_ref)
    p_refs = (p0_ref, p1_ref)
    lane = lax.broadcasted_iota(jnp.int32, (T, HEAD_DIM), 1)
    cq = jnp.sum(jnp.where(lane == h, cq_ref[...], 0.0), axis=-1, keepdims=True)
    cqr_ref[...] = jnp.broadcast_to(cq, (T, HEAD_DIM))
    m_ref[...] = jnp.full((T, HEAD_DIM), NEG, F32)
    l_ref[...] = jnp.zeros((T, HEAD_DIM), F32)
    acc_ref[...] = jnp.zeros((T, HEAD_DIM), F32)
    p1_ref[...] = jnp.zeros((T, T), BF16)

    def keys(j):
        return pl.ds(pl.multiple_of(j * T, T), T)

    def logits(j, slot):
        s_refs[slot][...] = _dot_nt(q_ref[...], k_ref[keys(j), :])

    def softmax(j, slot, diagonal):
        s_ref, p_ref = s_refs[slot], p_refs[slot]
        start = pl.multiple_of(j * T, T)
        for rc in range(T // R):
            rows = slice(rc * R, (rc + 1) * R)
            cq_r = cqr_ref[rows, :]
            sg = []
            for g in range(G):
                lo = g * HEAD_DIM
                if diagonal and lo > rc * R + R - 1:
                    sg.append(None)
                    continue
                t = s_ref[rows, lo:lo + HEAD_DIM] + cq_r - ck_ref[:, pl.ds(start + lo, HEAD_DIM)]
                if diagonal and lo + HEAD_DIM - 1 > rc * R:
                    rel = (lax.broadcasted_iota(jnp.int32, (R, HEAD_DIM), 0) - lax.broadcasted_iota(jnp.int32, (R, HEAD_DIM), 1))
                    t = jnp.where(rel >= lo - rc * R, t, NEG)
                sg.append(t)
            live = [t for t in sg if t is not None]
            m_old = m_ref[rows, :]
            m_new = jnp.maximum(m_old, jnp.max(functools.reduce(jnp.maximum, live), axis=-1, keepdims=True))
            alpha = jnp.exp2(m_old - m_new)
            ps = [None if t is None else jnp.exp2(t - m_new) for t in sg]
            row_sum = jnp.sum(functools.reduce(jnp.add, [p for p in ps if p is not None]), axis=-1, keepdims=True)
            l_ref[rows, :] = alpha * l_ref[rows, :] + row_sum
            m_ref[rows, :] = m_new
            alpha_ref[rows, :] = alpha
            for g in range(G):
                lo = g * HEAD_DIM
                p_ref[rows, lo:lo + HEAD_DIM] = (jnp.zeros((R, HEAD_DIM), BF16) if ps[g] is None else ps[g].astype(BF16))

    def stage(j, slot, diagonal):
        other = 1 - slot
        if not diagonal:
            logits(j + 1, other)
        pv = _dot(p_refs[other][...], v_ref[keys(jnp.maximum(j - 1, 0)), :])
        softmax(j, slot, diagonal)
        acc_ref[...] = (acc_ref[...] + pv) * alpha_ref[...]

    logits(0, 0)

    def pair(i, carry):
        stage(2 * i, 0, False)
        stage(2 * i + 1, 1, False)
        return carry

    lax.fori_loop(0, qb // 2, pair, 0)

    def finish(slot):
        stage(qb, slot, True)
        acc = acc_ref[...] + _dot(p_refs[slot][...], v_ref[keys(qb), :])
        o_ref[...] = (acc / l_ref[...]).astype(o_ref.dtype)

    @pl.when(qb % 2 == 0)
    def _():
        finish(0)

    @pl.when(qb % 2 == 1)
    def _():
        stage(qb - 1, 0, False)
        finish(1)


def _forgetting_attention(qkv, c, ct):
    s = qkv.shape[0]
    T = FOX_TQ
    q0 = 3 * WIDTH_A // HEAD_DIM
    k0 = q0 + N_HEADS_B
    v0 = k0 + N_HEADS_B
    ck = ct.reshape(HEAD_DIM, 1, s)
    return pl.pallas_call(
        _fox_kernel,
        grid=(N_HEADS_B, s // T),
        in_specs=[
            pl.BlockSpec((T, HEAD_DIM), lambda h, i: (i, q0 + h)),
            pl.BlockSpec((s, HEAD_DIM), lambda h, i: (0, k0 + h)),
            pl.BlockSpec((s, HEAD_DIM), lambda h, i: (0, v0 + h)),
            pl.BlockSpec((T, HEAD_DIM), lambda h, i: (i, 0)),
            pl.BlockSpec((None, 1, s), lambda h, i: (h, 0, 0)),
        ],
        out_specs=pl.BlockSpec((T, HEAD_DIM), lambda h, i: (i, h)),
        out_shape=jax.ShapeDtypeStruct((s, WIDTH_B), BF16),
        scratch_shapes=[
            pltpu.VMEM((T, T), F32), pltpu.VMEM((T, T), F32), pltpu.VMEM((T, T), BF16), pltpu.VMEM((T, T), BF16),
            pltpu.VMEM((T, HEAD_DIM), F32), pltpu.VMEM((T, HEAD_DIM), F32), pltpu.VMEM((T, HEAD_DIM), F32),
            pltpu.VMEM((T, HEAD_DIM), F32), pltpu.VMEM((T, HEAD_DIM), F32),
        ],
        compiler_params=_cparams(("parallel", "arbitrary")),
        name="fox_attention",
    )(qkv, qkv, qkv, c, ck)


def _gate_cast_kernel(main_ref, extra_ref, o_ref, *, row_off):
    x = jnp.concatenate([main_ref[...], extra_ref[...]], axis=0)
    o_ref[...] = x[row_off:row_off + GATE_TN, :].astype(o_ref.dtype)


def _gate_weights(w_in_t, g_lo):
    d = w_in_t.shape[1]
    blk0, row_off = divmod(g_lo, GATE_TN)
    assert row_off <= GATE_PAD and GATE_TN % GATE_PAD == 0 and g_lo + 2 * d <= w_in_t.shape[0]
    return pl.pallas_call(
        functools.partial(_gate_cast_kernel, row_off=row_off),
        grid=(2 * d // GATE_TN,),
        in_specs=[
            pl.BlockSpec((GATE_TN, d), lambda j: (blk0 + j, 0)),
            pl.BlockSpec((GATE_PAD, d), lambda j: ((blk0 + j + 1) * (GATE_TN // GATE_PAD), 0)),
        ],
        out_specs=pl.BlockSpec((GATE_TN, d), lambda j: (j, 0)),
        out_shape=jax.ShapeDtypeStruct((2 * d, d), BF16),
        compiler_params=_cparams(("parallel",), big=True),
        name="gate_weights",
    )(w_in_t, w_in_t)


def _up_kernel(xn_ref, ya_ref, yb_ref, wga_ref, wgb_ref, wua_ref, wub_ref, v_ref, o_ref, vt_ref):
    vt_ref[...] = v_ref[...].T.astype(vt_ref.dtype)
    xn = xn_ref[...]
    ga = jax.nn.sigmoid(_dot_nt(xn, wga_ref[...]))
    ua = _dot(ya_ref[...], wua_ref[...])
    part = ga * ua
    gb = jax.nn.sigmoid(_dot_nt(xn, wgb_ref[...]))
    ub = _dot(yb_ref[...], wub_ref[...])
    o_ref[...] = (part + gb * ub).astype(o_ref.dtype)


def _gated_up(xn, ya, yb, w_gates, w_up_a, w_up_b, peer_v):
    s, d = xn.shape
    n_i, n_j = s // UP_TM, d // UP_TN
    v_rows = peer_v.shape[0] // (n_i * n_j)
    assert v_rows * n_i * n_j == peer_v.shape[0] and v_rows % HEAD_DIM == 0
    row = lambda width: pl.BlockSpec((UP_TM, width), lambda i, j: (i, 0))
    col = lambda depth: pl.BlockSpec((depth, UP_TN), lambda i, j: (0, j))
    gate_a = pl.BlockSpec((UP_TN, d), lambda i, j: (j, 0))
    gate_b = pl.BlockSpec((UP_TN, d), lambda i, j: (j + d // UP_TN, 0))
    return pl.pallas_call(
        _up_kernel,
        grid=(s // UP_TM, d // UP_TN),
        in_specs=[row(d), row(WIDTH_A_OUT), row(WIDTH_B), gate_a, gate_b, col(WIDTH_A_OUT), col(WIDTH_B),
                  pl.BlockSpec((v_rows, d), lambda i, j: (i * n_j + j, 0))],
        out_specs=[pl.BlockSpec((UP_TM, UP_TN), lambda i, j: (i, j)),
                   pl.BlockSpec((d, v_rows), lambda i, j: (0, i * n_j + j))],
        out_shape=[jax.ShapeDtypeStruct((s, d), BF16), jax.ShapeDtypeStruct((d, peer_v.shape[0]), BF16)],
        compiler_params=_cparams(("parallel", "arbitrary"), big=True),
        name="gated_up",
    )(xn, ya, yb, w_gates, w_gates, w_up_a, w_up_b, peer_v)


def _out_kernel(a_ref, w_ref, x_ref, u_ref, o_ref, ub_ref):
    ub_ref[...] = u_ref[...].astype(ub_ref.dtype)
    o_ref[...] = x_ref[...] + _dot(a_ref[...], w_ref[...])


def _out_proj(merged, w_out, x2d, peer_u):
    s, d = x2d.shape
    n_i, n_j = s // OUT_TM, d // OUT_TN
    u_rows = peer_u.shape[0] // (n_i * n_j)
    assert u_rows * n_i * n_j == peer_u.shape[0] and u_rows % PACK == 0
    return pl.pallas_call(
        _out_kernel,
        grid=(n_i, n_j),
        in_specs=[
            pl.BlockSpec((OUT_TM, d), lambda i, j: (i, 0)),
            pl.BlockSpec((d, OUT_TN), lambda i, j: (0, j)),
            pl.BlockSpec((OUT_TM, OUT_TN), lambda i, j: (i, j)),
            pl.BlockSpec((u_rows, d), lambda i, j: (i * n_j + j, 0)),
        ],
        out_specs=[pl.BlockSpec((OUT_TM, OUT_TN), lambda i, j: (i, j)),
                   pl.BlockSpec((u_rows, d), lambda i, j: (i * n_j + j, 0))],
        out_shape=[jax.ShapeDtypeStruct((s, d), F32), jax.ShapeDtypeStruct(peer_u.shape, BF16)],
        compiler_params=_cparams(("parallel", "arbitrary"), big=True),
        name="out_proj",
    )(merged, w_out, x2d, peer_u)


def _peer_scores_kernel(wq_ref, hn_ref, sk_ref, o_ref):
    qt = _dot_nt(wq_ref[...], hn_ref[...]).astype(BF16)
    for r in range(PEER_HEADS * 2):
        rows = slice(r * N_KEYS, (r + 1) * N_KEYS)
        o_ref[rows, :] = _dot(sk_ref[rows, :], qt[rows, :])


def _peer_scores(hn, wq_t, sk):
    s, d = hn.shape
    rows = PEER_HEADS * 2 * N_KEYS
    assert PEER_QDIM // 2 == N_KEYS
    once = pl.Buffered(1)
    return pl.pallas_call(
        _peer_scores_kernel,
        grid=(s // PQ_TM,),
        in_specs=[
            pl.BlockSpec((PEER_HEADS * PEER_QDIM, d), lambda i: (0, 0), pipeline_mode=once),
            pl.BlockSpec((PQ_TM, d), lambda i: (i, 0)),
            pl.BlockSpec((rows, PEER_QDIM // 2), lambda i: (0, 0), pipeline_mode=once),
        ],
        out_specs=pl.BlockSpec((rows, PQ_TM), lambda i: (0, i)),
        out_shape=jax.ShapeDtypeStruct((rows, s), F32),
        compiler_params=_cparams(("parallel",), big=True),
        name="peer_scores",
    )(wq_t, hn, sk)


def _stair_width(a):
    return PEER_TOPK // (a + 1)


def _topk_kernel(sc_ref, e1_ref, w2_ref, tau_ref):
    K = PEER_TOPK
    s1 = sc_ref[0:N_KEYS, :]
    s2 = sc_ref[N_KEYS:2 * N_KEYS, :]
    ninf = -jnp.inf

    def top(cur, count):
        outs = []
        for _ in range(count):
            mk = jnp.max(cur, axis=0, keepdims=True)
            outs.append(mk)
            cur = jnp.where(cur == mk, ninf, cur)
        return outs

    t1 = top(s1, K)
    t2 = top(s2, K)
    t2_all = jnp.concatenate(t2, axis=0)
    rank = lax.broadcasted_iota(jnp.int32, t2_all.shape, 0)
    n_wide = K // 2
    blocks = [jnp.where(rank < _stair_width(a), t1[a] + t2_all, ninf) for a in range(n_wide)]
    blocks.append(jnp.concatenate(t1[n_wide:], axis=0) + t2[0])
    cand = jnp.concatenate(blocks, axis=0)
    thr = top(cand, K)[K - 1]
    z = jnp.sum(jnp.where(cand >= thr, jnp.exp(cand - (t1[0] + t2[0])), 0.0), axis=0, keepdims=True)
    e1_ref[...] = jnp.exp(s1 - t1[0])
    w2_ref[...] = jnp.exp(s2 - t2[0]) / z
    tau = jnp.full(s1.shape, jnp.inf, F32)
    for b in range(K):
        tau = jnp.where(s1 + t2[b] >= thr, t2[b], tau)
    tau_ref[...] = tau


def _peer_select(sc_t):
    s = sc_t.shape[1]
    rows = PEER_HEADS * N_KEYS
    spec = pl.BlockSpec((N_KEYS, TOPK_TL), lambda i, j: (j, i))
    sds = jax.ShapeDtypeStruct((rows, s), F32)
    return pl.pallas_call(
        _topk_kernel,
        grid=(s // TOPK_TL, PEER_HEADS),
        in_specs=[pl.BlockSpec((2 * N_KEYS, TOPK_TL), lambda i, j: (j, i))],
        out_specs=[spec, spec, spec],
        out_shape=[sds, sds, sds],
        compiler_params=_cparams(("parallel", "arbitrary")),
        name="peer_select",
    )(sc_t)


def _peer_kernel(hn_ref, u_ref, vt_ref, s2_ref, tau_ref, e1_ref, w2_ref, h_ref, o_ref, acc_ref, a0_ref, a1_ref,
                 ag0_ref, ag1_ref):
    ej = pl.program_id(1)
    HT = PEER_TM // 2
    keys_per_tile = PEER_TE // N_KEYS
    lane_chunks = HT // PEER_LC
    n_chunks = keys_per_tile * lane_chunks
    d = u_ref.shape[1]
    kc_size = d // n_chunks
    row_size = d // n_chunks

    @pl.when(ej == 0)
    def _():
        acc_ref[...] = jnp.zeros_like(acc_ref)

    inv_sqrt2 = 0.7071067811865476

    def gate_chunk(a_ref, ag_ref, half, c):
        b, lc = divmod(c, lane_chunks)
        i1 = ej * keys_per_tile + b
        rows = slice(b * N_KEYS, (b + 1) * N_KEYS)
        cols = slice(lc * PEER_LC, (lc + 1) * PEER_LC)
        tok = slice(half * HT + lc * PEER_LC, half * HT + (lc + 1) * PEER_LC)
        g = jnp.zeros((N_KEYS, PEER_LC), F32)
        for h in range(PEER_HEADS):
            tau = tau_ref[i1, h:h + 1, tok]
            e1 = e1_ref[i1, h:h + 1, tok]
            g = g + jnp.where(s2_ref[h, :, tok] >= tau, w2_ref[h * N_KEYS:(h + 1) * N_KEYS, tok], 0.0) * e1
        a = a_ref[rows, cols]
        act = 0.5 * a * (1.0 + lax.erf(a * inv_sqrt2))
        ag_ref[rows, cols] = (act * g).astype(BF16)

    a0_ref[...] = _dot_nt(u_ref[...], hn_ref[0:HT, :])
    for c in range(n_chunks):
        kc = slice(c * kc_size, (c + 1) * kc_size)
        piece = _dot_nt(u_ref[:, kc], hn_ref[HT:PEER_TM, kc])
        if c == 0:
            a1_ref[...] = piece
        else:
            a1_ref[...] += piece
        gate_chunk(a0_ref, ag0_ref, 0, c)
    for c in range(n_chunks):
        rows = slice(c * row_size, (c + 1) * row_size)
        acc_ref[rows, 0:HT] += _dot(vt_ref[rows, :], ag0_ref[...])
        gate_chunk(a1_ref, ag1_ref, 1, c)
    acc_ref[:, HT:PEER_TM] += _dot(vt_ref[...], ag1_ref[...])

    @pl.when(ej == pl.num_programs(1) - 1)
    def _():
        for c in range(d // PEER_TM):
            cols = slice(c * PEER_TM, (c + 1) * PEER_TM)
            o_ref[:, cols] = h_ref[:, cols] + acc_ref[cols, :].T


def _peer_dense(hn, u, vt, sc_t, e1, w2, tau, h):
    s, d = hn.shape
    once = pl.Buffered(1)
    tau_r = tau.reshape(PEER_HEADS, N_KEYS, s).transpose(1, 0, 2)
    e1_r = e1.reshape(PEER_HEADS, N_KEYS, s).transpose(1, 0, 2)
    by_key = pl.BlockSpec((N_KEYS, PEER_HEADS, PEER_TM), lambda i, j: (0, 0, i), pipeline_mode=once)
    sc4 = sc_t.reshape(PEER_HEADS, 2, N_KEYS, s)
    half = PEER_TM // 2
    return pl.pallas_call(
        _peer_kernel,
        grid=(s // PEER_TM, N_EXPERTS // PEER_TE),
        in_specs=[
            pl.BlockSpec((PEER_TM, d), lambda i, j: (i, 0), pipeline_mode=once),
            pl.BlockSpec((PEER_TE, d), lambda i, j: (j, 0)),
            pl.BlockSpec((d, PEER_TE), lambda i, j: (0, j)),
            pl.BlockSpec((PEER_HEADS, None, N_KEYS, PEER_TM), lambda i, j: (0, 1, 0, i), pipeline_mode=once),
            by_key,
            by_key,
            pl.BlockSpec((PEER_HEADS * N_KEYS, PEER_TM), lambda i, j: (0, i), pipeline_mode=once),
            pl.BlockSpec((PEER_TM, d), lambda i, j: (i, 0), pipeline_mode=once),
        ],
        out_specs=pl.BlockSpec((PEER_TM, d), lambda i, j: (i, 0), pipeline_mode=once),
        out_shape=jax.ShapeDtypeStruct((s, d), F32),
        scratch_shapes=[pltpu.VMEM((d, PEER_TM), F32),
                        pltpu.VMEM((PEER_TE, half), F32), pltpu.VMEM((PEER_TE, half), F32),
                        pltpu.VMEM((PEER_TE, half), BF16), pltpu.VMEM((PEER_TE, half), BF16)],
        compiler_params=_cparams(("parallel", "arbitrary"), big=True),
        name="peer_dense",
    )(hn, u, vt, sc4, tau_r, e1_r, w2, h)


def _layer(h, norm1_gain, w_in, b_forget, q_norm_a, k_norm_a, q_norm_b, k_norm_b,
           w_up_a, w_up_b, w_out, norm2_gain, w_peer_q, peer_subkeys, peer_u, peer_v):
    d = D_MODEL
    scale = HEAD_DIM ** -0.5
    f_lo = QKV_COLS
    g_lo = f_lo + N_HEADS_B
    b_f = jnp.pad(b_forget.astype(F32), (0, HEAD_DIM - N_HEADS_B)).reshape(1, HEAD_DIM)
    w_in_t = w_in.T
    w_gates = _gate_weights(w_in_t, g_lo)
    ones = jnp.ones((HEAD_DIM,), F32)
    gain = jnp.concatenate([
        jnp.tile(q_norm_a.astype(F32) * (scale * LOG2E), N_HEADS_A), jnp.tile(k_norm_a.astype(F32), N_HEADS_A), jnp.tile(ones, N_HEADS_A),
        jnp.tile(q_norm_b.astype(F32) * (scale * LOG2E), N_HEADS_B), jnp.tile(k_norm_b.astype(F32), N_HEADS_B), jnp.tile(ones, N_HEADS_B),
    ]).reshape(1, QKV_COLS)
    flag = jnp.concatenate([
        jnp.ones((2 * WIDTH_A,), F32), jnp.zeros((WIDTH_A,), F32), jnp.ones((2 * WIDTH_B,), F32), jnp.zeros((WIDTH_B,), F32),
    ]).reshape(1, QKV_COLS)

    xn = _rmsnorm_bf16(h, norm1_gain)
    qkv = _qkv_proj(xn, w_in_t, gain, flag)
    c, ct = _forget_cumsum(xn, w_in_t, b_f)
    y_a = _dilated_mixture(qkv)
    y_b = _forgetting_attention(qkv, c, ct)
    merged, v_t = _gated_up(xn, y_a, y_b, w_gates, w_up_a.astype(BF16), w_up_b.astype(BF16), peer_v)
    h, u_b = _out_proj(merged, w_out.astype(BF16), h, peer_u)

    hn = _rmsnorm_bf16(h, norm2_gain)
    wq_t = w_peer_q.T.astype(BF16)
    sk = peer_subkeys.reshape(PEER_HEADS * 2 * N_KEYS, PEER_QDIM // 2).astype(BF16)
    sc_t = _peer_scores(hn, wq_t, sk)
    e1, w2, tau = _peer_select(sc_t)
    return _peer_dense(hn, u_b, v_t, sc_t, e1, w2, tau, h)


def kernel(x, norm1_gain, w_in, b_forget, q_norm_a, k_norm_a, q_norm_b, k_norm_b,
           w_up_a, w_up_b, w_out, norm2_gain, w_peer_q, peer_subkeys, peer_u, peer_v):
    b, s, d = x.shape
    assert b == 1 and d == D_MODEL and s % (DILATION_GROUPS[-1][0]) == 0
    h = x.reshape(s, d)
    for layer in range(norm1_gain.shape[0]):
        h = _layer(h, norm1_gain[layer], w_in[layer], b_forget[layer], q_norm_a[layer], k_norm_a[layer],
                   q_norm_b[layer], k_norm_b[layer], w_up_a[layer], w_up_b[layer], w_out[layer], norm2_gain[layer],
                   w_peer_q[layer], peer_subkeys[layer], peer_u[layer], peer_v[layer])
    return h.reshape(b, s, d)
```

```python
import functools

import numpy as np
import jax
import jax.numpy as jnp
from jax import lax
from jax.experimental import pallas as pl
from jax.experimental.pallas import tpu as pltpu

F32 = jnp.float32
BF16 = jnp.bfloat16

D_MODEL = 4096
HEAD_DIM = 128
DILATION_GROUPS = ((128, 1), (512, 4), (2048, 16))
A_SLOTS = 6
N_HEADS_A = A_SLOTS * len(DILATION_GROUPS)
N_HEADS_B = D_MODEL // HEAD_DIM - N_HEADS_A
WIDTH_A = N_HEADS_A * HEAD_DIM
WIDTH_B = N_HEADS_B * HEAD_DIM
WIDTH_A_OUT = A_SLOTS * HEAD_DIM
QKV_COLS = 3 * WIDTH_A + 3 * WIDTH_B
ALIBI_MAX_EXP = 8.0
PEER_HEADS = 8
N_KEYS = 128
N_EXPERTS = N_KEYS * N_KEYS
PEER_TOPK = 16
PEER_QDIM = 256
EPS = 1e-6
NEG = -1e30
LOG2E = 1.4426950408889634

VMEM_LIMIT_BYTES = 56 * 1024 * 1024

NORM_TM = 512
QKV_TM, QKV_TN = 1024, 512
FORGET_TM = 256
FOX_TQ = 512
FOX_ROWS = 16
FOX_UNROLL = 4
DIL_T = 256
DIL_ROWS = 8
GATE_TN, GATE_PAD = 512, 16
UP_TM, UP_TN = 512, 512
OUT_TM, OUT_TN = 1024, 512
PQ_TM = 512
TOPK_TL = 256
PEER_TM, PEER_TE = 512, 512
PEER_LC = 128
PACK = 16


def _cparams(sem, big=False):
    return pltpu.CompilerParams(dimension_semantics=sem, vmem_limit_bytes=VMEM_LIMIT_BYTES if big else None)


def _dot(a, b):
    return jnp.dot(a, b, preferred_element_type=F32)


def _dot_nt(a, b):
    return lax.dot_general(a, b, (((1,), (1,)), ((), ())), preferred_element_type=F32)


def _rmsnorm_kernel(x_ref, g_ref, o_ref):
    x = x_ref[...]
    ms = jnp.mean(x * x, axis=-1, keepdims=True)
    o_ref[...] = (x * lax.rsqrt(ms + EPS) * g_ref[...]).astype(o_ref.dtype)


def _rmsnorm_bf16(x2d, gain):
    s, d = x2d.shape
    return pl.pallas_call(
        _rmsnorm_kernel,
        grid=(s // NORM_TM,),
        in_specs=[pl.BlockSpec((NORM_TM, d), lambda i: (i, 0)), pl.BlockSpec((1, d), lambda i: (0, 0))],
        out_specs=pl.BlockSpec((NORM_TM, d), lambda i: (i, 0)),
        out_shape=jax.ShapeDtypeStruct((s, d), BF16),
        compiler_params=_cparams(("parallel",)),
        name="rmsnorm",
    )(x2d, gain.reshape(1, d))


def _qkv_kernel(a_ref, w_ref, gain_ref, flag_ref, o_ref, wb_ref):
    @pl.when(pl.program_id(1) == 0)
    def _():
        wb_ref[...] = w_ref[...].astype(BF16)

    acc = _dot_nt(a_ref[...], wb_ref[...])
    for c in range(QKV_TN // HEAD_DIM):
        sl = slice(c * HEAD_DIM, (c + 1) * HEAD_DIM)
        y = acc[:, sl]
        ms = jnp.mean(y * y, axis=-1, keepdims=True)
        yn = y * lax.rsqrt(ms + EPS) * gain_ref[:, sl]
        o_ref[:, sl] = jnp.where(flag_ref[:, sl] > 0.0, yn, y).astype(o_ref.dtype)


def _qkv_proj(xn, w_in_t, gain, flag):
    s, d = xn.shape
    n = gain.shape[1]
    return pl.pallas_call(
        _qkv_kernel,
        grid=(n // QKV_TN, s // QKV_TM),
        in_specs=[
            pl.BlockSpec((QKV_TM, d), lambda j, i: (i, 0)),
            pl.BlockSpec((QKV_TN, d), lambda j, i: (j, 0)),
            pl.BlockSpec((1, QKV_TN), lambda j, i: (0, j)),
            pl.BlockSpec((1, QKV_TN), lambda j, i: (0, j)),
        ],
        out_specs=pl.BlockSpec((QKV_TM, QKV_TN), lambda j, i: (i, j)),
        out_shape=jax.ShapeDtypeStruct((s, n), BF16),
        scratch_shapes=[pltpu.VMEM((QKV_TN, d), BF16)],
        compiler_params=_cparams(("parallel", "arbitrary"), big=True),
        name="qkv_proj",
    )(xn, w_in_t, gain, flag)


def _forget_kernel(a_ref, w_ref, b_ref, c_ref, ct_ref, carry_ref):
    @pl.when(pl.program_id(0) == 0)
    def _():
        carry_ref[...] = jnp.zeros_like(carry_ref)

    f = _dot_nt(a_ref[...], w_ref[...].astype(BF16)) + b_ref[...]
    logf = jnp.minimum(f, 0.0) - jnp.log1p(jnp.exp(-jnp.abs(f)))
    r = lax.broadcasted_iota(jnp.int32, (FORGET_TM, FORGET_TM), 0)
    c = lax.broadcasted_iota(jnp.int32, (FORGET_TM, FORGET_TM), 1)
    tri = (c <= r).astype(F32)
    cs = jnp.dot(tri, logf, precision=lax.Precision.HIGHEST, preferred_element_type=F32) + carry_ref[...]
    cs2 = cs * LOG2E
    c_ref[...] = cs2
    ct_ref[...] = cs2.T
    carry_ref[...] = cs[FORGET_TM - 1:FORGET_TM, :]


def _forget_cumsum(xn, w_in_t, b_f):
    s, d = xn.shape
    assert QKV_COLS % HEAD_DIM == 0
    return pl.pallas_call(
        _forget_kernel,
        grid=(s // FORGET_TM,),
        in_specs=[
            pl.BlockSpec((FORGET_TM, d), lambda i: (i, 0)),
            pl.BlockSpec((HEAD_DIM, d), lambda i: (QKV_COLS // HEAD_DIM, 0)),
            pl.BlockSpec((1, HEAD_DIM), lambda i: (0, 0)),
        ],
        out_specs=[pl.BlockSpec((FORGET_TM, HEAD_DIM), lambda i: (i, 0)), pl.BlockSpec((HEAD_DIM, FORGET_TM), lambda i: (0, i))],
        out_shape=[jax.ShapeDtypeStruct((s, HEAD_DIM), F32), jax.ShapeDtypeStruct((HEAD_DIM, s), F32)],
        scratch_shapes=[pltpu.VMEM((1, HEAD_DIM), F32)],
        compiler_params=_cparams(("arbitrary",)),
        name="forget_cumsum",
    )(xn, w_in_t, b_f)


def _alibi_slopes():
    n = N_HEADS_A
    return np.exp2(-np.float32(ALIBI_MAX_EXP) * np.arange(1, n + 1, dtype=np.float32) / np.float32(n)).astype(np.float32)


def _dilated_tiles():
    tiles, first = [], []
    for g, (window, dilation) in enumerate(DILATION_GROUPS):
        first.append(len(tiles))
        for back in range((window + DIL_T - 1) // DIL_T + 1):
            tiles.append((g, dilation, window, back))
    return tiles, first


def _dilated_kernel(slope_ref, q0_ref, q1_ref, q2_ref, k0_ref, k1_ref, k2_ref, v0_ref, v1_ref, v2_ref, y_ref,
                    bm_ref, s0_ref, s1_ref, s2_ref, p0_ref, p1_ref, p2_ref, m_ref, l_ref, acc_ref, *, seq):
    i = pl.program_id(1)
    T = DIL_T
    R = DIL_ROWS
    tiles, first = _dilated_tiles()
    none_tile = len(tiles)
    q_refs = (q0_ref, q1_ref, q2_ref)
    k_refs = (k0_ref, k1_ref, k2_ref)
    v_refs = (v0_ref, v1_ref, v2_ref)
    s_refs = (s0_ref, s1_ref, s2_ref)
    p_refs = (p0_ref, p1_ref, p2_ref)

    @pl.when(i == 0)
    def _():
        base = lax.broadcasted_iota(jnp.int32, (T, T), 0) - lax.broadcasted_iota(jnp.int32, (T, T), 1)
        for idx, (g, dilation, window, back) in enumerate(tiles):
            rel = base + T * back
            ok = jnp.where((base & (dilation - 1)) == 0, rel, -1)
            ok = jnp.where(ok <= window, ok, -1)
            bm_ref[idx] = jnp.where(ok >= 0, -slope_ref[g:g + 1, :] * rel.astype(F32), NEG)
        bm_ref[none_tile] = jnp.full((T, T), NEG, F32)

    geom = []
    for g, (window, dilation) in enumerate(DILATION_GROUPS):
        n_back = (window + T - 1) // T
        n_cols = min(n_back + 1, seq // T)
        first_blk = jnp.clip(i - n_back, 0, seq // T - n_cols)
        strip = pl.ds(pl.multiple_of(first_blk * T, T), n_cols * T)
        geom.append((n_back, n_cols, first_blk, strip))
    for g, (n_back, n_cols, first_blk, strip) in enumerate(geom):
        s_refs[g][...] = _dot_nt(q_refs[g][...], k_refs[g][strip, :])
    for g, (n_back, n_cols, first_blk, strip) in enumerate(geom):
        s_ref, p_ref = s_refs[g], p_refs[g]
        tile_of = []
        for c in range(n_cols):
            back = i - (first_blk + c)
            tile_of.append(jnp.where((back >= 0) & (back <= n_back), first[g] + back, none_tile))
        lane_blocks = [(c, h) for c in range(n_cols) for h in range(T // HEAD_DIM)]
        for rc in range(T // R):
            rows = slice(rc * R, (rc + 1) * R)
            ts = [s_ref[rows, c * T + h * HEAD_DIM:c * T + (h + 1) * HEAD_DIM]
                  + bm_ref[tile_of[c], rows, h * HEAD_DIM:(h + 1) * HEAD_DIM] for c, h in lane_blocks]
            m = jnp.max(functools.reduce(jnp.maximum, ts), axis=-1, keepdims=True)
            ps = [jnp.exp2(t - m) for t in ts]
            l = jnp.sum(functools.reduce(jnp.add, ps), axis=-1, keepdims=True)
            m_ref[g, rows, :] = jnp.broadcast_to(m, (R, HEAD_DIM))
            l_ref[g, rows, :] = jnp.broadcast_to(l, (R, HEAD_DIM))
            for (c, h), p in zip(lane_blocks, ps):
                p_ref[rows, c * T + h * HEAD_DIM:c * T + (h + 1) * HEAD_DIM] = p.astype(BF16)
        acc_ref[g] = _dot(p_ref[...], v_refs[g][strip, :])

    ms = [m_ref[g] for g in range(len(DILATION_GROUPS))]
    m_star = functools.reduce(jnp.maximum, ms)
    num = 0.0
    den = 0.0
    for g, m in enumerate(ms):
        w = jnp.exp2(m - m_star)
        num = num + w * acc_ref[g]
        den = den + w * l_ref[g]
    y_ref[...] = (num / den).astype(y_ref.dtype)


def _dilated_mixture(qkv):
    s = qkv.shape[0]
    T = DIL_T
    n_groups = len(DILATION_GROUPS)
    tiles, _ = _dilated_tiles()
    cols = [min((window + T - 1) // T + 1, s // T) for window, _ in DILATION_GROUPS]
    slopes = (_alibi_slopes() * np.float32(LOG2E)).reshape(n_groups, A_SLOTS).T
    slopes = jnp.asarray(np.broadcast_to(slopes[:, :, None], (A_SLOTS, n_groups, T)).copy())
    k0 = N_HEADS_A
    v0 = 2 * N_HEADS_A
    q_specs = [pl.BlockSpec((T, HEAD_DIM), lambda j, i, g=g: (i, g * A_SLOTS + j)) for g in range(n_groups)]
    k_specs = [pl.BlockSpec((s, HEAD_DIM), lambda j, i, g=g: (0, k0 + g * A_SLOTS + j)) for g in range(n_groups)]
    v_specs = [pl.BlockSpec((s, HEAD_DIM), lambda j, i, g=g: (0, v0 + g * A_SLOTS + j)) for g in range(n_groups)]
    return pl.pallas_call(
        functools.partial(_dilated_kernel, seq=s),
        grid=(A_SLOTS, s // T),
        in_specs=[pl.BlockSpec((None, n_groups, T), lambda j, i: (j, 0, 0))] + q_specs + k_specs + v_specs,
        out_specs=pl.BlockSpec((T, HEAD_DIM), lambda j, i: (i, j)),
        out_shape=jax.ShapeDtypeStruct((s, WIDTH_A_OUT), BF16),
        scratch_shapes=[
            pltpu.VMEM((len(tiles) + 1, T, T), F32),
            *[pltpu.VMEM((T, c * T), F32) for c in cols], *[pltpu.VMEM((T, c * T), BF16) for c in cols],
            pltpu.VMEM((n_groups, T, HEAD_DIM), F32), pltpu.VMEM((n_groups, T, HEAD_DIM), F32),
            pltpu.VMEM((n_groups, T, HEAD_DIM), F32),
        ],
        compiler_params=_cparams(("parallel", "arbitrary"), big=True),
        name="dilated_mixture",
    )(slopes, *([qkv] * (3 * n_groups)))


def _fox_kernel(q_ref, k_ref, v_ref, cq_ref, ck_ref, o_ref, s0_ref, s1_ref, p0_ref, p1_ref, cqr_ref, m_ref, l_ref,
                alpha_ref, acc_ref):
    h = pl.program_id(0)
    qb = pl.program_id(1)
    T = FOX_TQ
    R = FOX_ROWS
    G = T // HEAD_DIM
    s_refs = (s0_ref, s1_ref)
    p_refs = (p0_ref, p1_ref)
    lane = lax.broadcasted_iota(jnp.int32, (T, HEAD_DIM), 1)
    cq = jnp.sum(jnp.where(lane == h, cq_ref[...], 0.0), axis=-1, keepdims=True)
    cqr_ref[...] = jnp.broadcast_to(cq, (T, HEAD_DIM))
    m_ref[...] = jnp.full((T, HEAD_DIM), NEG, F32)
    l_ref[...] = jnp.zeros((T, HEAD_DIM), F32)
    acc_ref[...] = jnp.zeros((T, HEAD_DIM), F32)
    p1_ref[...] = jnp.zeros((T, T), BF16)

    def keys(j):
        return pl.ds(pl.multiple_of(j * T, T), T)

    def logits(j, slot):
        s_refs[slot][...] = _dot_nt(q_ref[...], k_ref[keys(j), :])

    def softmax(j, slot, diagonal):
        s_ref, p_ref = s_refs[slot], p_refs[slot]
        start = pl.multiple_of(j * T, T)
        for rc in range(T // R):
            rows = slice(rc * R, (rc + 1) * R)
            cq_r = cqr_ref[rows, :]
            sg = []
            for g in range(G):
                lo = g * HEAD_DIM
                if diagonal and lo > rc * R + R - 1:
                    sg.append(None)
                    continue
                t = s_ref[rows, lo:lo + HEAD_DIM] + cq_r - ck_ref[:, pl.ds(start + lo, HEAD_DIM)]
                if diagonal and lo + HEAD_DIM - 1 > rc * R:
                    rel = (lax.broadcasted_iota(jnp.int32, (R, HEAD_DIM), 0) - lax.broadcasted_iota(jnp.int32, (R, HEAD_DIM), 1))
                    t = jnp.where(rel >= lo - rc * R, t, NEG)
                sg.append(t)
            live = [t for t in sg if t is not None]
            m_old = m_ref[rows, :]
            m_new = jnp.maximum(m_old, jnp.max(functools.reduce(jnp.maximum, live), axis=-1, keepdims=True))
            alpha = jnp.exp2(m_old - m_new)
            ps = [None if t is None else jnp.exp2(t - m_new) for t in sg]
            row_sum = jnp.sum(functools.reduce(jnp.add, [p for p in ps if p is not None]), axis=-1, keepdims=True)
            l_ref[rows, :] = alpha * l_ref[rows, :] + row_sum
            m_ref[rows, :] = m_new
            alpha_ref[rows, :] = alpha
            for g in range(G):
                lo = g * HEAD_DIM
                p_ref[rows, lo:lo + HEAD_DIM] = (jnp.zeros((R, HEAD_DIM), BF16) if ps[g] is None else ps[g].astype(BF16))

    def stage(j, slot, diagonal):
        other = 1 - slot
        if not diagonal:
            logits(j + 1, other)
        pv = _dot(p_refs[other][...], v_ref[keys(jnp.maximum(j - 1, 0)), :])
        softmax(j, slot, diagonal)
        acc_ref[...] = (acc_ref[...] + pv) * alpha_ref[...]

    logits(0, 0)

    def group(i, carry):
        for u in range(FOX_UNROLL):
            stage(FOX_UNROLL * i + u, u % 2, False)
        return carry

    lax.fori_loop(0, qb // FOX_UNROLL, group, 0)

    def finish(slot):
        stage(qb, slot, True)
        acc = acc_ref[...] + _dot(p_refs[slot][...], v_ref[keys(qb), :])
        o_ref[...] = (acc / l_ref[...]).astype(o_ref.dtype)

    for r in range(FOX_UNROLL):
        @pl.when(qb % FOX_UNROLL == r)
        def _(r=r):
            for u in range(r):
                stage(qb - r + u, u % 2, False)
            finish(r % 2)


def _forgetting_attention(qkv, c, ct):
    s = qkv.shape[0]
    T = FOX_TQ
    q0 = 3 * WIDTH_A // HEAD_DIM
    k0 = q0 + N_HEADS_B
    v0 = k0 + N_HEADS_B
    ck = ct.reshape(HEAD_DIM, 1, s)
    return pl.pallas_call(
        _fox_kernel,
        grid=(N_HEADS_B, s // T),
        in_specs=[
            pl.BlockSpec((T, HEAD_DIM), lambda h, i: (i, q0 + h)),
            pl.BlockSpec((s, HEAD_DIM), lambda h, i: (0, k0 + h)),
            pl.BlockSpec((s, HEAD_DIM), lambda h, i: (0, v0 + h)),
            pl.BlockSpec((T, HEAD_DIM), lambda h, i: (i, 0)),
            pl.BlockSpec((None, 1, s), lambda h, i: (h, 0, 0)),
        ],
        out_specs=pl.BlockSpec((T, HEAD_DIM), lambda h, i: (i, h)),
        out_shape=jax.ShapeDtypeStruct((s, WIDTH_B), BF16),
        scratch_shapes=[
            pltpu.VMEM((T, T), F32), pltpu.VMEM((T, T), F32), pltpu.VMEM((T, T), BF16), pltpu.VMEM((T, T), BF16),
            pltpu.VMEM((T, HEAD_DIM), F32), pltpu.VMEM((T, HEAD_DIM), F32), pltpu.VMEM((T, HEAD_DIM), F32),
            pltpu.VMEM((T, HEAD_DIM), F32), pltpu.VMEM((T, HEAD_DIM), F32),
        ],
        compiler_params=_cparams(("parallel", "arbitrary")),
        name="fox_attention",
    )(qkv, qkv, qkv, c, ck)


def _gate_cast_kernel(main_ref, extra_ref, o_ref, *, row_off):
    x = jnp.concatenate([main_ref[...], extra_ref[...]], axis=0)
    o_ref[...] = x[row_off:row_off + GATE_TN, :].astype(o_ref.dtype)


def _gate_weights(w_in_t, g_lo):
    d = w_in_t.shape[1]
    blk0, row_off = divmod(g_lo, GATE_TN)
    assert row_off <= GATE_PAD and GATE_TN % GATE_PAD == 0 and g_lo + 2 * d <= w_in_t.shape[0]
    return pl.pallas_call(
        functools.partial(_gate_cast_kernel, row_off=row_off),
        grid=(2 * d // GATE_TN,),
        in_specs=[
            pl.BlockSpec((GATE_TN, d), lambda j: (blk0 + j, 0)),
            pl.BlockSpec((GATE_PAD, d), lambda j: ((blk0 + j + 1) * (GATE_TN // GATE_PAD), 0)),
        ],
        out_specs=pl.BlockSpec((GATE_TN, d), lambda j: (j, 0)),
        out_shape=jax.ShapeDtypeStruct((2 * d, d), BF16),
        compiler_params=_cparams(("parallel",), big=True),
        name="gate_weights",
    )(w_in_t, w_in_t)


def _up_kernel(xn_ref, ya_ref, yb_ref, wga_ref, wgb_ref, wua_ref, wub_ref, v_ref, wo_ref, o_ref, vt_ref, wob_ref):
    vt_ref[...] = v_ref[...].T.astype(vt_ref.dtype)
    wob_ref[...] = wo_ref[...].astype(wob_ref.dtype)
    xn = xn_ref[...]
    ga = jax.nn.sigmoid(_dot_nt(xn, wga_ref[...]))
    ua = _dot(ya_ref[...], wua_ref[...])
    part = ga * ua
    gb = jax.nn.sigmoid(_dot_nt(xn, wgb_ref[...]))
    ub = _dot(yb_ref[...], wub_ref[...])
    o_ref[...] = (part + gb * ub).astype(o_ref.dtype)


def _gated_up(xn, ya, yb, w_gates, w_up_a, w_up_b, peer_v, w_out):
    s, d = xn.shape
    n_i, n_j = s // UP_TM, d // UP_TN
    v_rows = peer_v.shape[0] // (n_i * n_j)
    wo_rows = w_out.shape[0] // (n_i * n_j)
    assert v_rows * n_i * n_j == peer_v.shape[0] and v_rows % HEAD_DIM == 0
    assert wo_rows * n_i * n_j == w_out.shape[0] and wo_rows % PACK == 0
    row = lambda width: pl.BlockSpec((UP_TM, width), lambda i, j: (i, 0))
    col = lambda depth: pl.BlockSpec((depth, UP_TN), lambda i, j: (0, j))
    gate_a = pl.BlockSpec((UP_TN, d), lambda i, j: (j, 0))
    gate_b = pl.BlockSpec((UP_TN, d), lambda i, j: (j + d // UP_TN, 0))
    return pl.pallas_call(
        _up_kernel,
        grid=(s // UP_TM, d // UP_TN),
        in_specs=[row(d), row(WIDTH_A_OUT), row(WIDTH_B), gate_a, gate_b, col(WIDTH_A_OUT), col(WIDTH_B),
                  pl.BlockSpec((v_rows, d), lambda i, j: (i * n_j + j, 0)),
                  pl.BlockSpec((wo_rows, d), lambda i, j: (i * n_j + j, 0))],
        out_specs=[pl.BlockSpec((UP_TM, UP_TN), lambda i, j: (i, j)),
                   pl.BlockSpec((d, v_rows), lambda i, j: (0, i * n_j + j)),
                   pl.BlockSpec((wo_rows, d), lambda i, j: (i * n_j + j, 0))],
        out_shape=[jax.ShapeDtypeStruct((s, d), BF16), jax.ShapeDtypeStruct((d, peer_v.shape[0]), BF16),
                   jax.ShapeDtypeStruct(w_out.shape, BF16)],
        compiler_params=_cparams(("parallel", "arbitrary"), big=True),
        name="gated_up",
    )(xn, ya, yb, w_gates, w_gates, w_up_a, w_up_b, peer_v, w_out)


def _out_kernel(a_ref, w_ref, x_ref, u_ref, o_ref, ub_ref):
    ub_ref[...] = u_ref[...].astype(ub_ref.dtype)
    o_ref[...] = x_ref[...] + _dot(a_ref[...], w_ref[...])


def _out_proj(merged, w_out, x2d, peer_u):
    s, d = x2d.shape
    n_i, n_j = s // OUT_TM, d // OUT_TN
    u_rows = peer_u.shape[0] // (n_i * n_j)
    assert u_rows * n_i * n_j == peer_u.shape[0] and u_rows % PACK == 0
    return pl.pallas_call(
        _out_kernel,
        grid=(n_i, n_j),
        in_specs=[
            pl.BlockSpec((OUT_TM, d), lambda i, j: (i, 0)),
            pl.BlockSpec((d, OUT_TN), lambda i, j: (0, j)),
            pl.BlockSpec((OUT_TM, OUT_TN), lambda i, j: (i, j)),
            pl.BlockSpec((u_rows, d), lambda i, j: (i * n_j + j, 0)),
        ],
        out_specs=[pl.BlockSpec((OUT_TM, OUT_TN), lambda i, j: (i, j)),
                   pl.BlockSpec((u_rows, d), lambda i, j: (i * n_j + j, 0))],
        out_shape=[jax.ShapeDtypeStruct((s, d), F32), jax.ShapeDtypeStruct(peer_u.shape, BF16)],
        compiler_params=_cparams(("parallel", "arbitrary"), big=True),
        name="out_proj",
    )(merged, w_out, x2d, peer_u)


def _peer_scores_kernel(wq_ref, hn_ref, sk_ref, o_ref):
    qt = _dot_nt(wq_ref[...], hn_ref[...]).astype(BF16)
    for r in range(PEER_HEADS * 2):
        rows = slice(r * N_KEYS, (r + 1) * N_KEYS)
        o_ref[rows, :] = _dot(sk_ref[rows, :], qt[rows, :])


def _peer_scores(hn, wq_t, sk):
    s, d = hn.shape
    rows = PEER_HEADS * 2 * N_KEYS
    assert PEER_QDIM // 2 == N_KEYS
    once = pl.Buffered(1)
    return pl.pallas_call(
        _peer_scores_kernel,
        grid=(s // PQ_TM,),
        in_specs=[
            pl.BlockSpec((PEER_HEADS * PEER_QDIM, d), lambda i: (0, 0), pipeline_mode=once),
            pl.BlockSpec((PQ_TM, d), lambda i: (i, 0)),
            pl.BlockSpec((rows, PEER_QDIM // 2), lambda i: (0, 0), pipeline_mode=once),
        ],
        out_specs=pl.BlockSpec((rows, PQ_TM), lambda i: (0, i)),
        out_shape=jax.ShapeDtypeStruct((rows, s), F32),
        compiler_params=_cparams(("parallel",), big=True),
        name="peer_scores",
    )(wq_t, hn, sk)


def _stair_width(a):
    return PEER_TOPK // (a + 1)


def _topk_kernel(sc_ref, e1_ref, w2_ref, tau_ref):
    K = PEER_TOPK
    s1 = sc_ref[0:N_KEYS, :]
    s2 = sc_ref[N_KEYS:2 * N_KEYS, :]
    ninf = -jnp.inf

    def top(cur, count):
        outs = []
        for _ in range(count):
            mk = jnp.max(cur, axis=0, keepdims=True)
            outs.append(mk)
            cur = jnp.where(cur == mk, ninf, cur)
        return outs

    t1 = top(s1, K)
    t2 = top(s2, K)
    t2_all = jnp.concatenate(t2, axis=0)
    rank = lax.broadcasted_iota(jnp.int32, t2_all.shape, 0)
    n_wide = K // 2
    blocks = [jnp.where(rank < _stair_width(a), t1[a] + t2_all, ninf) for a in range(n_wide)]
    blocks.append(jnp.concatenate(t1[n_wide:], axis=0) + t2[0])
    cand = jnp.concatenate(blocks, axis=0)
    thr = top(cand, K)[K - 1]
    z = jnp.sum(jnp.where(cand >= thr, jnp.exp(cand - (t1[0] + t2[0])), 0.0), axis=0, keepdims=True)
    e1_ref[...] = jnp.exp(s1 - t1[0])
    w2_ref[...] = jnp.exp(s2 - t2[0]) / z
    tau = jnp.full(s1.shape, jnp.inf, F32)
    for b in range(K):
        tau = jnp.where(s1 + t2[b] >= thr, t2[b], tau)
    tau_ref[...] = tau


def _peer_select(sc_t):
    s = sc_t.shape[1]
    rows = PEER_HEADS * N_KEYS
    spec = pl.BlockSpec((N_KEYS, TOPK_TL), lambda i, j: (j, i))
    sds = jax.ShapeDtypeStruct((rows, s), F32)
    return pl.pallas_call(
        _topk_kernel,
        grid=(s // TOPK_TL, PEER_HEADS),
        in_specs=[pl.BlockSpec((2 * N_KEYS, TOPK_TL), lambda i, j: (j, i))],
        out_specs=[spec, spec, spec],
        out_shape=[sds, sds, sds],
        compiler_params=_cparams(("parallel", "arbitrary")),
        name="peer_select",
    )(sc_t)


def _peer_kernel(hn_ref, u_ref, vt_ref, s2_ref, tau_ref, e1_ref, w2_ref, h_ref, o_ref, acc_ref, a0_ref, a1_ref,
                 ag0_ref, ag1_ref):
    ej = pl.program_id(1)
    HT = PEER_TM // 2
    keys_per_tile = PEER_TE // N_KEYS
    lane_chunks = HT // PEER_LC
    n_chunks = keys_per_tile * lane_chunks
    d = u_ref.shape[1]
    kc_size = d // n_chunks
    row_size = d // n_chunks

    @pl.when(ej == 0)
    def _():
        acc_ref[...] = jnp.zeros_like(acc_ref)

    inv_sqrt2 = 0.7071067811865476

    def gate_chunk(a_ref, ag_ref, half, c):
        b, lc = divmod(c, lane_chunks)
        i1 = ej * keys_per_tile + b
        rows = slice(b * N_KEYS, (b + 1) * N_KEYS)
        cols = slice(lc * PEER_LC, (lc + 1) * PEER_LC)
        tok = slice(half * HT + lc * PEER_LC, half * HT + (lc + 1) * PEER_LC)
        g = jnp.zeros((N_KEYS, PEER_LC), F32)
        for h in range(PEER_HEADS):
            tau = tau_ref[i1, h:h + 1, tok]
            e1 = e1_ref[i1, h:h + 1, tok]
            g = g + jnp.where(s2_ref[h, :, tok] >= tau, w2_ref[h * N_KEYS:(h + 1) * N_KEYS, tok], 0.0) * e1
        a = a_ref[rows, cols]
        act = 0.5 * a * (1.0 + lax.erf(a * inv_sqrt2))
        ag_ref[rows, cols] = (act * g).astype(BF16)

    a0_ref[...] = _dot_nt(u_ref[...], hn_ref[0:HT, :])
    for c in range(n_chunks):
        kc = slice(c * kc_size, (c + 1) * kc_size)
        piece = _dot_nt(u_ref[:, kc], hn_ref[HT:PEER_TM, kc])
        if c == 0:
            a1_ref[...] = piece
        else:
            a1_ref[...] += piece
        gate_chunk(a0_ref, ag0_ref, 0, c)
    for c in range(n_chunks):
        rows = slice(c * row_size, (c + 1) * row_size)
        acc_ref[rows, 0:HT] += _dot(vt_ref[rows, :], ag0_ref[...])
        gate_chunk(a1_ref, ag1_ref, 1, c)
    acc_ref[:, HT:PEER_TM] += _dot(vt_ref[...], ag1_ref[...])

    @pl.when(ej == pl.num_programs(1) - 1)
    def _():
        for c in range(d // PEER_TM):
            cols = slice(c * PEER_TM, (c + 1) * PEER_TM)
            o_ref[:, cols] = h_ref[:, cols] + acc_ref[cols, :].T


def _peer_dense(hn, u, vt, sc_t, e1, w2, tau, h):
    s, d = hn.shape
    once = pl.Buffered(1)
    tau_r = tau.reshape(PEER_HEADS, N_KEYS, s).transpose(1, 0, 2)
    e1_r = e1.reshape(PEER_HEADS, N_KEYS, s).transpose(1, 0, 2)
    by_key = pl.BlockSpec((N_KEYS, PEER_HEADS, PEER_TM), lambda i, j: (0, 0, i), pipeline_mode=once)
    sc4 = sc_t.reshape(PEER_HEADS, 2, N_KEYS, s)
    half = PEER_TM // 2
    return pl.pallas_call(
        _peer_kernel,
        grid=(s // PEER_TM, N_EXPERTS // PEER_TE),
        in_specs=[
            pl.BlockSpec((PEER_TM, d), lambda i, j: (i, 0), pipeline_mode=once),
            pl.BlockSpec((PEER_TE, d), lambda i, j: (j, 0)),
            pl.BlockSpec((d, PEER_TE), lambda i, j: (0, j)),
            pl.BlockSpec((PEER_HEADS, None, N_KEYS, PEER_TM), lambda i, j: (0, 1, 0, i), pipeline_mode=once),
            by_key,
            by_key,
            pl.BlockSpec((PEER_HEADS * N_KEYS, PEER_TM), lambda i, j: (0, i), pipeline_mode=once),
            pl.BlockSpec((PEER_TM, d), lambda i, j: (i, 0), pipeline_mode=once),
        ],
        out_specs=pl.BlockSpec((PEER_TM, d), lambda i, j: (i, 0), pipeline_mode=once),
        out_shape=jax.ShapeDtypeStruct((s, d), F32),
        scratch_shapes=[pltpu.VMEM((d, PEER_TM), F32),
                        pltpu.VMEM((PEER_TE, half), F32), pltpu.VMEM((PEER_TE, half), F32),
                        pltpu.VMEM((PEER_TE, half), BF16), pltpu.VMEM((PEER_TE, half), BF16)],
        compiler_params=_cparams(("parallel", "arbitrary"), big=True),
        name="peer_dense",
    )(hn, u, vt, sc4, tau_r, e1_r, w2, h)


def _layer(h, norm1_gain, w_in, b_forget, q_norm_a, k_norm_a, q_norm_b, k_norm_b,
           w_up_a, w_up_b, w_out, norm2_gain, w_peer_q, peer_subkeys, peer_u, peer_v):
    d = D_MODEL
    scale = HEAD_DIM ** -0.5
    f_lo = QKV_COLS
    g_lo = f_lo + N_HEADS_B
    b_f = jnp.pad(b_forget.astype(F32), (0, HEAD_DIM - N_HEADS_B)).reshape(1, HEAD_DIM)
    w_in_t = w_in.T
    w_gates = _gate_weights(w_in_t, g_lo)
    ones = jnp.ones((HEAD_DIM,), F32)
    gain = jnp.concatenate([
        jnp.tile(q_norm_a.astype(F32) * (scale * LOG2E), N_HEADS_A), jnp.tile(k_norm_a.astype(F32), N_HEADS_A), jnp.tile(ones, N_HEADS_A),
        jnp.tile(q_norm_b.astype(F32) * (scale * LOG2E), N_HEADS_B), jnp.tile(k_norm_b.astype(F32), N_HEADS_B), jnp.tile(ones, N_HEADS_B),
    ]).reshape(1, QKV_COLS)
    flag = jnp.concatenate([
        jnp.ones((2 * WIDTH_A,), F32), jnp.zeros((WIDTH_A,), F32), jnp.ones((2 * WIDTH_B,), F32), jnp.zeros((WIDTH_B,), F32),
    ]).reshape(1, QKV_COLS)

    xn = _rmsnorm_bf16(h, norm1_gain)
    qkv = _qkv_proj(xn, w_in_t, gain, flag)
    c, ct = _forget_cumsum(xn, w_in_t, b_f)
    y_a = _dilated_mixture(qkv)
    y_b = _forgetting_attention(qkv, c, ct)
    merged, v_t, w_out_b = _gated_up(xn, y_a, y_b, w_gates, w_up_a.astype(BF16), w_up_b.astype(BF16), peer_v, w_out)
    h, u_b = _out_proj(merged, w_out_b, h, peer_u)

    hn = _rmsnorm_bf16(h, norm2_gain)
    wq_t = w_peer_q.T.astype(BF16)
    sk = peer_subkeys.reshape(PEER_HEADS * 2 * N_KEYS, PEER_QDIM // 2).astype(BF16)
    sc_t = _peer_scores(hn, wq_t, sk)
    e1, w2, tau = _peer_select(sc_t)
    return _peer_dense(hn, u_b, v_t, sc_t, e1, w2, tau, h)


def kernel(x, norm1_gain, w_in, b_forget, q_norm_a, k_norm_a, q_norm_b, k_norm_b,
           w_up_a, w_up_b, w_out, norm2_gain, w_peer_q, peer_subkeys, peer_u, peer_v):
    b, s, d = x.shape
    assert b == 1 and d == D_MODEL and s % (DILATION_GROUPS[-1][0]) == 0
    h = x.reshape(s, d)
    for layer in range(norm1_gain.shape[0]):
        h = _layer(h, norm1_gain[layer], w_in[layer], b_forget[layer], q_norm_a[layer], k_norm_a[layer],
                   q_norm_b[layer], k_norm_b[layer], w_up_a[layer], w_up_b[layer], w_out[layer], norm2_gain[layer],
                   w_peer_q[layer], peer_subkeys[layer], peer_u[layer], peer_v[layer])
    return h.reshape(b, s, d)
```

```python
import functools

import numpy as np
import jax
import jax.numpy as jnp
from jax import lax
from jax.experimental import pallas as pl
from jax.experimental.pallas import tpu as pltpu

F32 = jnp.float32
BF16 = jnp.bfloat16

D_MODEL = 4096
HEAD_DIM = 128
DILATION_GROUPS = ((128, 1), (512, 4), (2048, 16))
A_SLOTS = 6
N_HEADS_A = A_SLOTS * len(DILATION_GROUPS)
N_HEADS_B = D_MODEL // HEAD_DIM - N_HEADS_A
WIDTH_A = N_HEADS_A * HEAD_DIM
WIDTH_B = N_HEADS_B * HEAD_DIM
WIDTH_A_OUT = A_SLOTS * HEAD_DIM
QKV_COLS = 3 * WIDTH_A + 3 * WIDTH_B
ALIBI_MAX_EXP = 8.0
PEER_HEADS = 8
N_KEYS = 128
N_EXPERTS = N_KEYS * N_KEYS
PEER_TOPK = 16
PEER_QDIM = 256
EPS = 1e-6
NEG = -1e30
LOG2E = 1.4426950408889634

VMEM_LIMIT_BYTES = 56 * 1024 * 1024

NORM_TM = 512
QKV_TM, QKV_TN = 1024, 512
FORGET_TM = 256
FOX_TQ = 512
FOX_ROWS = 16
FOX_UNROLL = 8
DIL_T = 256
DIL_ROWS = 8
GATE_TN, GATE_PAD = 512, 16
UP_TM, UP_TN = 512, 512
OUT_TM, OUT_TN = 1024, 512
PQ_TM = 512
TOPK_TL = 256
PEER_TM, PEER_TE = 512, 512
PEER_LC = 128
PACK = 16


def _cparams(sem, big=False):
    return pltpu.CompilerParams(dimension_semantics=sem, vmem_limit_bytes=VMEM_LIMIT_BYTES if big else None)


def _dot(a, b):
    return jnp.dot(a, b, preferred_element_type=F32)


def _dot_nt(a, b):
    return lax.dot_general(a, b, (((1,), (1,)), ((), ())), preferred_element_type=F32)


def _rmsnorm_kernel(x_ref, g_ref, o_ref):
    x = x_ref[...]
    ms = jnp.mean(x * x, axis=-1, keepdims=True)
    o_ref[...] = (x * lax.rsqrt(ms + EPS) * g_ref[...]).astype(o_ref.dtype)


def _rmsnorm_bf16(x2d, gain):
    s, d = x2d.shape
    return pl.pallas_call(
        _rmsnorm_kernel,
        grid=(s // NORM_TM,),
        in_specs=[pl.BlockSpec((NORM_TM, d), lambda i: (i, 0)), pl.BlockSpec((1, d), lambda i: (0, 0))],
        out_specs=pl.BlockSpec((NORM_TM, d), lambda i: (i, 0)),
        out_shape=jax.ShapeDtypeStruct((s, d), BF16),
        compiler_params=_cparams(("parallel",)),
        name="rmsnorm",
    )(x2d, gain.reshape(1, d))


def _qkv_kernel(a_ref, w_ref, gain_ref, flag_ref, o_ref, wb_ref):
    @pl.when(pl.program_id(1) == 0)
    def _():
        wb_ref[...] = w_ref[...].astype(BF16)

    acc = _dot_nt(a_ref[...], wb_ref[...])
    for c in range(QKV_TN // HEAD_DIM):
        sl = slice(c * HEAD_DIM, (c + 1) * HEAD_DIM)
        y = acc[:, sl]
        ms = jnp.mean(y * y, axis=-1, keepdims=True)
        yn = y * lax.rsqrt(ms + EPS) * gain_ref[:, sl]
        o_ref[:, sl] = jnp.where(flag_ref[:, sl] > 0.0, yn, y).astype(o_ref.dtype)


def _qkv_proj(xn, w_in_t, gain, flag):
    s, d = xn.shape
    n = gain.shape[1]
    return pl.pallas_call(
        _qkv_kernel,
        grid=(n // QKV_TN, s // QKV_TM),
        in_specs=[
            pl.BlockSpec((QKV_TM, d), lambda j, i: (i, 0)),
            pl.BlockSpec((QKV_TN, d), lambda j, i: (j, 0)),
            pl.BlockSpec((1, QKV_TN), lambda j, i: (0, j)),
            pl.BlockSpec((1, QKV_TN), lambda j, i: (0, j)),
        ],
        out_specs=pl.BlockSpec((QKV_TM, QKV_TN), lambda j, i: (i, j)),
        out_shape=jax.ShapeDtypeStruct((s, n), BF16),
        scratch_shapes=[pltpu.VMEM((QKV_TN, d), BF16)],
        compiler_params=_cparams(("parallel", "arbitrary"), big=True),
        name="qkv_proj",
    )(xn, w_in_t, gain, flag)


def _forget_kernel(a_ref, w_ref, b_ref, c_ref, ct_ref, carry_ref):
    @pl.when(pl.program_id(0) == 0)
    def _():
        carry_ref[...] = jnp.zeros_like(carry_ref)

    f = _dot_nt(a_ref[...], w_ref[...].astype(BF16)) + b_ref[...]
    logf = jnp.minimum(f, 0.0) - jnp.log1p(jnp.exp(-jnp.abs(f)))
    r = lax.broadcasted_iota(jnp.int32, (FORGET_TM, FORGET_TM), 0)
    c = lax.broadcasted_iota(jnp.int32, (FORGET_TM, FORGET_TM), 1)
    tri = (c <= r).astype(F32)
    cs = jnp.dot(tri, logf, precision=lax.Precision.HIGHEST, preferred_element_type=F32) + carry_ref[...]
    cs2 = cs * LOG2E
    c_ref[...] = cs2
    ct_ref[...] = cs2.T
    carry_ref[...] = cs[FORGET_TM - 1:FORGET_TM, :]


def _forget_cumsum(xn, w_in_t, b_f):
    s, d = xn.shape
    assert QKV_COLS % HEAD_DIM == 0
    return pl.pallas_call(
        _forget_kernel,
        grid=(s // FORGET_TM,),
        in_specs=[
            pl.BlockSpec((FORGET_TM, d), lambda i: (i, 0)),
            pl.BlockSpec((HEAD_DIM, d), lambda i: (QKV_COLS // HEAD_DIM, 0)),
            pl.BlockSpec((1, HEAD_DIM), lambda i: (0, 0)),
        ],
        out_specs=[pl.BlockSpec((FORGET_TM, HEAD_DIM), lambda i: (i, 0)), pl.BlockSpec((HEAD_DIM, FORGET_TM), lambda i: (0, i))],
        out_shape=[jax.ShapeDtypeStruct((s, HEAD_DIM), F32), jax.ShapeDtypeStruct((HEAD_DIM, s), F32)],
        scratch_shapes=[pltpu.VMEM((1, HEAD_DIM), F32)],
        compiler_params=_cparams(("arbitrary",)),
        name="forget_cumsum",
    )(xn, w_in_t, b_f)


def _alibi_slopes():
    n = N_HEADS_A
    return np.exp2(-np.float32(ALIBI_MAX_EXP) * np.arange(1, n + 1, dtype=np.float32) / np.float32(n)).astype(np.float32)


def _dilated_tiles():
    tiles, first = [], []
    for g, (window, dilation) in enumerate(DILATION_GROUPS):
        first.append(len(tiles))
        for back in range((window + DIL_T - 1) // DIL_T + 1):
            tiles.append((g, dilation, window, back))
    return tiles, first


def _dilated_kernel(slope_ref, q0_ref, q1_ref, q2_ref, k0_ref, k1_ref, k2_ref, v0_ref, v1_ref, v2_ref, y_ref,
                    bm_ref, s0_ref, s1_ref, s2_ref, p0_ref, p1_ref, p2_ref, m_ref, l_ref, acc_ref, *, seq):
    i = pl.program_id(1)
    T = DIL_T
    R = DIL_ROWS
    tiles, first = _dilated_tiles()
    none_tile = len(tiles)
    q_refs = (q0_ref, q1_ref, q2_ref)
    k_refs = (k0_ref, k1_ref, k2_ref)
    v_refs = (v0_ref, v1_ref, v2_ref)
    s_refs = (s0_ref, s1_ref, s2_ref)
    p_refs = (p0_ref, p1_ref, p2_ref)

    @pl.when(i == 0)
    def _():
        base = lax.broadcasted_iota(jnp.int32, (T, T), 0) - lax.broadcasted_iota(jnp.int32, (T, T), 1)
        for idx, (g, dilation, window, back) in enumerate(tiles):
            rel = base + T * back
            ok = jnp.where((base & (dilation - 1)) == 0, rel, -1)
            ok = jnp.where(ok <= window, ok, -1)
            bm_ref[idx] = jnp.where(ok >= 0, -slope_ref[g:g + 1, :] * rel.astype(F32), NEG)
        bm_ref[none_tile] = jnp.full((T, T), NEG, F32)

    geom = []
    for g, (window, dilation) in enumerate(DILATION_GROUPS):
        n_back = (window + T - 1) // T
        n_cols = min(n_back + 1, seq // T)
        first_blk = jnp.clip(i - n_back, 0, seq // T - n_cols)
        strip = pl.ds(pl.multiple_of(first_blk * T, T), n_cols * T)
        geom.append((n_back, n_cols, first_blk, strip))
    for g, (n_back, n_cols, first_blk, strip) in enumerate(geom):
        s_refs[g][...] = _dot_nt(q_refs[g][...], k_refs[g][strip, :])
    for g, (n_back, n_cols, first_blk, strip) in enumerate(geom):
        s_ref, p_ref = s_refs[g], p_refs[g]
        tile_of = []
        for c in range(n_cols):
            back = i - (first_blk + c)
            tile_of.append(jnp.where((back >= 0) & (back <= n_back), first[g] + back, none_tile))
        lane_blocks = [(c, h) for c in range(n_cols) for h in range(T // HEAD_DIM)]
        for rc in range(T // R):
            rows = slice(rc * R, (rc + 1) * R)
            ts = [s_ref[rows, c * T + h * HEAD_DIM:c * T + (h + 1) * HEAD_DIM]
                  + bm_ref[tile_of[c], rows, h * HEAD_DIM:(h + 1) * HEAD_DIM] for c, h in lane_blocks]
            m = jnp.max(functools.reduce(jnp.maximum, ts), axis=-1, keepdims=True)
            ps = [jnp.exp2(t - m) for t in ts]
            l = jnp.sum(functools.reduce(jnp.add, ps), axis=-1, keepdims=True)
            m_ref[g, rows, :] = jnp.broadcast_to(m, (R, HEAD_DIM))
            l_ref[g, rows, :] = jnp.broadcast_to(l, (R, HEAD_DIM))
            for (c, h), p in zip(lane_blocks, ps):
                p_ref[rows, c * T + h * HEAD_DIM:c * T + (h + 1) * HEAD_DIM] = p.astype(BF16)
        acc_ref[g] = _dot(p_ref[...], v_refs[g][strip, :])

    ms = [m_ref[g] for g in range(len(DILATION_GROUPS))]
    m_star = functools.reduce(jnp.maximum, ms)
    num = 0.0
    den = 0.0
    for g, m in enumerate(ms):
        w = jnp.exp2(m - m_star)
        num = num + w * acc_ref[g]
        den = den + w * l_ref[g]
    y_ref[...] = (num / den).astype(y_ref.dtype)


def _dilated_mixture(qkv):
    s = qkv.shape[0]
    T = DIL_T
    n_groups = len(DILATION_GROUPS)
    tiles, _ = _dilated_tiles()
    cols = [min((window + T - 1) // T + 1, s // T) for window, _ in DILATION_GROUPS]
    slopes = (_alibi_slopes() * np.float32(LOG2E)).reshape(n_groups, A_SLOTS).T
    slopes = jnp.asarray(np.broadcast_to(slopes[:, :, None], (A_SLOTS, n_groups, T)).copy())
    k0 = N_HEADS_A
    v0 = 2 * N_HEADS_A
    q_specs = [pl.BlockSpec((T, HEAD_DIM), lambda j, i, g=g: (i, g * A_SLOTS + j)) for g in range(n_groups)]
    k_specs = [pl.BlockSpec((s, HEAD_DIM), lambda j, i, g=g: (0, k0 + g * A_SLOTS + j)) for g in range(n_groups)]
    v_specs = [pl.BlockSpec((s, HEAD_DIM), lambda j, i, g=g: (0, v0 + g * A_SLOTS + j)) for g in range(n_groups)]
    return pl.pallas_call(
        functools.partial(_dilated_kernel, seq=s),
        grid=(A_SLOTS, s // T),
        in_specs=[pl.BlockSpec((None, n_groups, T), lambda j, i: (j, 0, 0))] + q_specs + k_specs + v_specs,
        out_specs=pl.BlockSpec((T, HEAD_DIM), lambda j, i: (i, j)),
        out_shape=jax.ShapeDtypeStruct((s, WIDTH_A_OUT), BF16),
        scratch_shapes=[
            pltpu.VMEM((len(tiles) + 1, T, T), F32),
            *[pltpu.VMEM((T, c * T), F32) for c in cols], *[pltpu.VMEM((T, c * T), BF16) for c in cols],
            pltpu.VMEM((n_groups, T, HEAD_DIM), F32), pltpu.VMEM((n_groups, T, HEAD_DIM), F32),
            pltpu.VMEM((n_groups, T, HEAD_DIM), F32),
        ],
        compiler_params=_cparams(("parallel", "arbitrary"), big=True),
        name="dilated_mixture",
    )(slopes, *([qkv] * (3 * n_groups)))


def _fox_kernel(q_ref, k_ref, v_ref, cq_ref, ck_ref, o_ref, s0_ref, s1_ref, p0_ref, p1_ref, cqr_ref, m_ref, l_ref,
                alpha_ref, acc_ref):
    h = pl.program_id(0)
    qb = pl.program_id(1)
    T = FOX_TQ
    R = FOX_ROWS
    G = T // HEAD_DIM
    s_refs = (s0_ref, s1_ref)
    p_refs = (p0_ref, p1_ref)
    lane = lax.broadcasted_iota(jnp.int32, (T, HEAD_DIM), 1)
    cq = jnp.sum(jnp.where(lane == h, cq_ref[...], 0.0), axis=-1, keepdims=True)
    cqr_ref[...] = jnp.broadcast_to(cq, (T, HEAD_DIM))
    m_ref[...] = jnp.full((T, HEAD_DIM), NEG, F32)
    l_ref[...] = jnp.zeros((T, HEAD_DIM), F32)
    acc_ref[...] = jnp.zeros((T, HEAD_DIM), F32)
    p1_ref[...] = jnp.zeros((T, T), BF16)

    def keys(j):
        return pl.ds(pl.multiple_of(j * T, T), T)

    def logits(j, slot):
        s_refs[slot][...] = _dot_nt(q_ref[...], k_ref[keys(j), :])

    def softmax(j, slot, diagonal):
        s_ref, p_ref = s_refs[slot], p_refs[slot]
        start = pl.multiple_of(j * T, T)
        for rc in range(T // R):
            rows = slice(rc * R, (rc + 1) * R)
            cq_r = cqr_ref[rows, :]
            sg = []
            for g in range(G):
                lo = g * HEAD_DIM
                if diagonal and lo > rc * R + R - 1:
                    sg.append(None)
                    continue
                t = s_ref[rows, lo:lo + HEAD_DIM] + cq_r - ck_ref[:, pl.ds(start + lo, HEAD_DIM)]
                if diagonal and lo + HEAD_DIM - 1 > rc * R:
                    rel = (lax.broadcasted_iota(jnp.int32, (R, HEAD_DIM), 0) - lax.broadcasted_iota(jnp.int32, (R, HEAD_DIM), 1))
                    t = jnp.where(rel >= lo - rc * R, t, NEG)
                sg.append(t)
            live = [t for t in sg if t is not None]
            m_old = m_ref[rows, :]
            m_new = jnp.maximum(m_old, jnp.max(functools.reduce(jnp.maximum, live), axis=-1, keepdims=True))
            alpha = jnp.exp2(m_old - m_new)
            ps = [None if t is None else jnp.exp2(t - m_new) for t in sg]
            row_sum = jnp.sum(functools.reduce(jnp.add, [p for p in ps if p is not None]), axis=-1, keepdims=True)
            l_ref[rows, :] = alpha * l_ref[rows, :] + row_sum
            m_ref[rows, :] = m_new
            alpha_ref[rows, :] = alpha
            for g in range(G):
                lo = g * HEAD_DIM
                p_ref[rows, lo:lo + HEAD_DIM] = (jnp.zeros((R, HEAD_DIM), BF16) if ps[g] is None else ps[g].astype(BF16))

    def stage(j, slot, diagonal):
        other = 1 - slot
        if not diagonal:
            logits(j + 1, other)
        pv = _dot(p_refs[other][...], v_ref[keys(jnp.maximum(j - 1, 0)), :])
        softmax(j, slot, diagonal)
        acc_ref[...] = (acc_ref[...] + pv) * alpha_ref[...]

    logits(0, 0)

    def group(i, carry):
        for u in range(FOX_UNROLL):
            stage(FOX_UNROLL * i + u, u % 2, False)
        return carry

    lax.fori_loop(0, qb // FOX_UNROLL, group, 0)

    def finish(slot):
        stage(qb, slot, True)
        acc = acc_ref[...] + _dot(p_refs[slot][...], v_ref[keys(qb), :])
        o_ref[...] = (acc / l_ref[...]).astype(o_ref.dtype)

    for r in range(FOX_UNROLL):
        @pl.when(qb % FOX_UNROLL == r)
        def _(r=r):
            for u in range(r):
                stage(qb - r + u, u % 2, False)
            finish(r % 2)


def _forgetting_attention(qkv, c, ct):
    s = qkv.shape[0]
    T = FOX_TQ
    q0 = 3 * WIDTH_A // HEAD_DIM
    k0 = q0 + N_HEADS_B
    v0 = k0 + N_HEADS_B
    ck = ct.reshape(HEAD_DIM, 1, s)
    return pl.pallas_call(
        _fox_kernel,
        grid=(N_HEADS_B, s // T),
        in_specs=[
            pl.BlockSpec((T, HEAD_DIM), lambda h, i: (i, q0 + h)),
            pl.BlockSpec((s, HEAD_DIM), lambda h, i: (0, k0 + h)),
            pl.BlockSpec((s, HEAD_DIM), lambda h, i: (0, v0 + h)),
            pl.BlockSpec((T, HEAD_DIM), lambda h, i: (i, 0)),
            pl.BlockSpec((None, 1, s), lambda h, i: (h, 0, 0)),
        ],
        out_specs=pl.BlockSpec((T, HEAD_DIM), lambda h, i: (i, h)),
        out_shape=jax.ShapeDtypeStruct((s, WIDTH_B), BF16),
        scratch_shapes=[
            pltpu.VMEM((T, T), F32), pltpu.VMEM((T, T), F32), pltpu.VMEM((T, T), BF16), pltpu.VMEM((T, T), BF16),
            pltpu.VMEM((T, HEAD_DIM), F32), pltpu.VMEM((T, HEAD_DIM), F32), pltpu.VMEM((T, HEAD_DIM), F32),
            pltpu.VMEM((T, HEAD_DIM), F32), pltpu.VMEM((T, HEAD_DIM), F32),
        ],
        compiler_params=_cparams(("parallel", "arbitrary")),
        name="fox_attention",
    )(qkv, qkv, qkv, c, ck)


def _gate_cast_kernel(main_ref, extra_ref, o_ref, *, row_off):
    x = jnp.concatenate([main_ref[...], extra_ref[...]], axis=0)
    o_ref[...] = x[row_off:row_off + GATE_TN, :].astype(o_ref.dtype)


def _gate_weights(w_in_t, g_lo):
    d = w_in_t.shape[1]
    blk0, row_off = divmod(g_lo, GATE_TN)
    assert row_off <= GATE_PAD and GATE_TN % GATE_PAD == 0 and g_lo + 2 * d <= w_in_t.shape[0]
    return pl.pallas_call(
        functools.partial(_gate_cast_kernel, row_off=row_off),
        grid=(2 * d // GATE_TN,),
        in_specs=[
            pl.BlockSpec((GATE_TN, d), lambda j: (blk0 + j, 0)),
            pl.BlockSpec((GATE_PAD, d), lambda j: ((blk0 + j + 1) * (GATE_TN // GATE_PAD), 0)),
        ],
        out_specs=pl.BlockSpec((GATE_TN, d), lambda j: (j, 0)),
        out_shape=jax.ShapeDtypeStruct((2 * d, d), BF16),
        compiler_params=_cparams(("parallel",), big=True),
        name="gate_weights",
    )(w_in_t, w_in_t)


def _up_kernel(xn_ref, ya_ref, yb_ref, wga_ref, wgb_ref, wua_ref, wub_ref, v_ref, wo_ref, o_ref, vt_ref, wob_ref):
    vt_ref[...] = v_ref[...].T.astype(vt_ref.dtype)
    wob_ref[...] = wo_ref[...].astype(wob_ref.dtype)
    xn = xn_ref[...]
    ga = jax.nn.sigmoid(_dot_nt(xn, wga_ref[...]))
    ua = _dot(ya_ref[...], wua_ref[...])
    part = ga * ua
    gb = jax.nn.sigmoid(_dot_nt(xn, wgb_ref[...]))
    ub = _dot(yb_ref[...], wub_ref[...])
    o_ref[...] = (part + gb * ub).astype(o_ref.dtype)


def _gated_up(xn, ya, yb, w_gates, w_up_a, w_up_b, peer_v, w_out):
    s, d = xn.shape
    n_i, n_j = s // UP_TM, d // UP_TN
    v_rows = peer_v.shape[0] // (n_i * n_j)
    wo_rows = w_out.shape[0] // (n_i * n_j)
    assert v_rows * n_i * n_j == peer_v.shape[0] and v_rows % HEAD_DIM == 0
    assert wo_rows * n_i * n_j == w_out.shape[0] and wo_rows % PACK == 0
    row = lambda width: pl.BlockSpec((UP_TM, width), lambda i, j: (i, 0))
    col = lambda depth: pl.BlockSpec((depth, UP_TN), lambda i, j: (0, j))
    gate_a = pl.BlockSpec((UP_TN, d), lambda i, j: (j, 0))
    gate_b = pl.BlockSpec((UP_TN, d), lambda i, j: (j + d // UP_TN, 0))
    return pl.pallas_call(
        _up_kernel,
        grid=(s // UP_TM, d // UP_TN),
        in_specs=[row(d), row(WIDTH_A_OUT), row(WIDTH_B), gate_a, gate_b, col(WIDTH_A_OUT), col(WIDTH_B),
                  pl.BlockSpec((v_rows, d), lambda i, j: (i * n_j + j, 0)),
                  pl.BlockSpec((wo_rows, d), lambda i, j: (i * n_j + j, 0))],
        out_specs=[pl.BlockSpec((UP_TM, UP_TN), lambda i, j: (i, j)),
                   pl.BlockSpec((d, v_rows), lambda i, j: (0, i * n_j + j)),
                   pl.BlockSpec((wo_rows, d), lambda i, j: (i * n_j + j, 0))],
        out_shape=[jax.ShapeDtypeStruct((s, d), BF16), jax.ShapeDtypeStruct((d, peer_v.shape[0]), BF16),
                   jax.ShapeDtypeStruct(w_out.shape, BF16)],
        compiler_params=_cparams(("parallel", "arbitrary"), big=True),
        name="gated_up",
    )(xn, ya, yb, w_gates, w_gates, w_up_a, w_up_b, peer_v, w_out)


def _out_kernel(a_ref, w_ref, x_ref, u_ref, o_ref, ub_ref):
    ub_ref[...] = u_ref[...].astype(ub_ref.dtype)
    o_ref[...] = x_ref[...] + _dot(a_ref[...], w_ref[...])


def _out_proj(merged, w_out, x2d, peer_u):
    s, d = x2d.shape
    n_i, n_j = s // OUT_TM, d // OUT_TN
    u_rows = peer_u.shape[0] // (n_i * n_j)
    assert u_rows * n_i * n_j == peer_u.shape[0] and u_rows % PACK == 0
    return pl.pallas_call(
        _out_kernel,
        grid=(n_i, n_j),
        in_specs=[
            pl.BlockSpec((OUT_TM, d), lambda i, j: (i, 0)),
            pl.BlockSpec((d, OUT_TN), lambda i, j: (0, j)),
            pl.BlockSpec((OUT_TM, OUT_TN), lambda i, j: (i, j)),
            pl.BlockSpec((u_rows, d), lambda i, j: (i * n_j + j, 0)),
        ],
        out_specs=[pl.BlockSpec((OUT_TM, OUT_TN), lambda i, j: (i, j)),
                   pl.BlockSpec((u_rows, d), lambda i, j: (i * n_j + j, 0))],
        out_shape=[jax.ShapeDtypeStruct((s, d), F32), jax.ShapeDtypeStruct(peer_u.shape, BF16)],
        compiler_params=_cparams(("parallel", "arbitrary"), big=True),
        name="out_proj",
    )(merged, w_out, x2d, peer_u)


def _peer_scores_kernel(wq_ref, hn_ref, sk_ref, o_ref):
    qt = _dot_nt(wq_ref[...], hn_ref[...]).astype(BF16)
    for r in range(PEER_HEADS * 2):
        rows = slice(r * N_KEYS, (r + 1) * N_KEYS)
        o_ref[rows, :] = _dot(sk_ref[rows, :], qt[rows, :])


def _peer_scores(hn, wq_t, sk):
    s, d = hn.shape
    rows = PEER_HEADS * 2 * N_KEYS
    assert PEER_QDIM // 2 == N_KEYS
    once = pl.Buffered(1)
    return pl.pallas_call(
        _peer_scores_kernel,
        grid=(s // PQ_TM,),
        in_specs=[
            pl.BlockSpec((PEER_HEADS * PEER_QDIM, d), lambda i: (0, 0), pipeline_mode=once),
            pl.BlockSpec((PQ_TM, d), lambda i: (i, 0)),
            pl.BlockSpec((rows, PEER_QDIM // 2), lambda i: (0, 0), pipeline_mode=once),
        ],
        out_specs=pl.BlockSpec((rows, PQ_TM), lambda i: (0, i)),
        out_shape=jax.ShapeDtypeStruct((rows, s), F32),
        compiler_params=_cparams(("parallel",), big=True),
        name="peer_scores",
    )(wq_t, hn, sk)


def _stair_width(a):
    return PEER_TOPK // (a + 1)


def _topk_kernel(sc_ref, e1_ref, w2_ref, tau_ref):
    K = PEER_TOPK
    s1 = sc_ref[0:N_KEYS, :]
    s2 = sc_ref[N_KEYS:2 * N_KEYS, :]
    ninf = -jnp.inf

    def top(cur, count):
        outs = []
        for _ in range(count):
            mk = jnp.max(cur, axis=0, keepdims=True)
            outs.append(mk)
            cur = jnp.where(cur == mk, ninf, cur)
        return outs

    t1 = top(s1, K)
    t2 = top(s2, K)
    t2_all = jnp.concatenate(t2, axis=0)
    rank = lax.broadcasted_iota(jnp.int32, t2_all.shape, 0)
    n_wide = K // 2
    blocks = [jnp.where(rank < _stair_width(a), t1[a] + t2_all, ninf) for a in range(n_wide)]
    blocks.append(jnp.concatenate(t1[n_wide:], axis=0) + t2[0])
    cand = jnp.concatenate(blocks, axis=0)
    thr = top(cand, K)[K - 1]
    z = jnp.sum(jnp.where(cand >= thr, jnp.exp(cand - (t1[0] + t2[0])), 0.0), axis=0, keepdims=True)
    e1_ref[...] = jnp.exp(s1 - t1[0])
    w2_ref[...] = jnp.exp(s2 - t2[0]) / z
    tau = jnp.full(s1.shape, jnp.inf, F32)
    for b in range(K):
        tau = jnp.where(s1 + t2[b] >= thr, t2[b], tau)
    tau_ref[...] = tau


def _peer_select(sc_t):
    s = sc_t.shape[1]
    rows = PEER_HEADS * N_KEYS
    spec = pl.BlockSpec((N_KEYS, TOPK_TL), lambda i, j: (j, i))
    sds = jax.ShapeDtypeStruct((rows, s), F32)
    return pl.pallas_call(
        _topk_kernel,
        grid=(s // TOPK_TL, PEER_HEADS),
        in_specs=[pl.BlockSpec((2 * N_KEYS, TOPK_TL), lambda i, j: (j, i))],
        out_specs=[spec, spec, spec],
        out_shape=[sds, sds, sds],
        compiler_params=_cparams(("parallel", "arbitrary")),
        name="peer_select",
    )(sc_t)


def _peer_kernel(hn_ref, u_ref, vt_ref, s2_ref, tau_ref, e1_ref, w2_ref, h_ref, o_ref, acc_ref, a0_ref, a1_ref,
                 ag0_ref, ag1_ref):
    ej = pl.program_id(1)
    HT = PEER_TM // 2
    keys_per_tile = PEER_TE // N_KEYS
    lane_chunks = HT // PEER_LC
    n_chunks = keys_per_tile * lane_chunks
    d = u_ref.shape[1]
    kc_size = d // n_chunks
    row_size = d // n_chunks

    @pl.when(ej == 0)
    def _():
        acc_ref[...] = jnp.zeros_like(acc_ref)

    inv_sqrt2 = 0.7071067811865476

    def gate_chunk(a_ref, ag_ref, half, c):
        b, lc = divmod(c, lane_chunks)
        i1 = ej * keys_per_tile + b
        rows = slice(b * N_KEYS, (b + 1) * N_KEYS)
        cols = slice(lc * PEER_LC, (lc + 1) * PEER_LC)
        tok = slice(half * HT + lc * PEER_LC, half * HT + (lc + 1) * PEER_LC)
        g = jnp.zeros((N_KEYS, PEER_LC), F32)
        for h in range(PEER_HEADS):
            tau = tau_ref[i1, h:h + 1, tok]
            e1 = e1_ref[i1, h:h + 1, tok]
            g = g + jnp.where(s2_ref[h, :, tok] >= tau, w2_ref[h * N_KEYS:(h + 1) * N_KEYS, tok], 0.0) * e1
        a = a_ref[rows, cols]
        act = 0.5 * a * (1.0 + lax.erf(a * inv_sqrt2))
        ag_ref[rows, cols] = (act * g).astype(BF16)

    a0_ref[...] = _dot_nt(u_ref[...], hn_ref[0:HT, :])
    for c in range(n_chunks):
        kc = slice(c * kc_size, (c + 1) * kc_size)
        piece = _dot_nt(u_ref[:, kc], hn_ref[HT:PEER_TM, kc])
        if c == 0:
            a1_ref[...] = piece
        else:
            a1_ref[...] += piece
        gate_chunk(a0_ref, ag0_ref, 0, c)
    for c in range(n_chunks):
        rows = slice(c * row_size, (c + 1) * row_size)
        acc_ref[rows, 0:HT] += _dot(vt_ref[rows, :], ag0_ref[...])
        gate_chunk(a1_ref, ag1_ref, 1, c)
    acc_ref[:, HT:PEER_TM] += _dot(vt_ref[...], ag1_ref[...])

    @pl.when(ej == pl.num_programs(1) - 1)
    def _():
        for c in range(d // PEER_TM):
            cols = slice(c * PEER_TM, (c + 1) * PEER_TM)
            o_ref[:, cols] = h_ref[:, cols] + acc_ref[cols, :].T


def _peer_dense(hn, u, vt, sc_t, e1, w2, tau, h):
    s, d = hn.shape
    once = pl.Buffered(1)
    tau_r = tau.reshape(PEER_HEADS, N_KEYS, s).transpose(1, 0, 2)
    e1_r = e1.reshape(PEER_HEADS, N_KEYS, s).transpose(1, 0, 2)
    by_key = pl.BlockSpec((N_KEYS, PEER_HEADS, PEER_TM), lambda i, j: (0, 0, i), pipeline_mode=once)
    sc4 = sc_t.reshape(PEER_HEADS, 2, N_KEYS, s)
    half = PEER_TM // 2
    return pl.pallas_call(
        _peer_kernel,
        grid=(s // PEER_TM, N_EXPERTS // PEER_TE),
        in_specs=[
            pl.BlockSpec((PEER_TM, d), lambda i, j: (i, 0), pipeline_mode=once),
            pl.BlockSpec((PEER_TE, d), lambda i, j: (j, 0)),
            pl.BlockSpec((d, PEER_TE), lambda i, j: (0, j)),
            pl.BlockSpec((PEER_HEADS, None, N_KEYS, PEER_TM), lambda i, j: (0, 1, 0, i), pipeline_mode=once),
            by_key,
            by_key,
            pl.BlockSpec((PEER_HEADS * N_KEYS, PEER_TM), lambda i, j: (0, i), pipeline_mode=once),
            pl.BlockSpec((PEER_TM, d), lambda i, j: (i, 0), pipeline_mode=once),
        ],
        out_specs=pl.BlockSpec((PEER_TM, d), lambda i, j: (i, 0), pipeline_mode=once),
        out_shape=jax.ShapeDtypeStruct((s, d), F32),
        scratch_shapes=[pltpu.VMEM((d, PEER_TM), F32),
                        pltpu.VMEM((PEER_TE, half), F32), pltpu.VMEM((PEER_TE, half), F32),
                        pltpu.VMEM((PEER_TE, half), BF16), pltpu.VMEM((PEER_TE, half), BF16)],
        compiler_params=_cparams(("parallel", "arbitrary"), big=True),
        name="peer_dense",
    )(hn, u, vt, sc4, tau_r, e1_r, w2, h)


def _layer(h, norm1_gain, w_in, b_forget, q_norm_a, k_norm_a, q_norm_b, k_norm_b,
           w_up_a, w_up_b, w_out, norm2_gain, w_peer_q, peer_subkeys, peer_u, peer_v):
    d = D_MODEL
    scale = HEAD_DIM ** -0.5
    f_lo = QKV_COLS
    g_lo = f_lo + N_HEADS_B
    b_f = jnp.pad(b_forget.astype(F32), (0, HEAD_DIM - N_HEADS_B)).reshape(1, HEAD_DIM)
    w_in_t = w_in.T
    w_gates = _gate_weights(w_in_t, g_lo)
    ones = jnp.ones((HEAD_DIM,), F32)
    gain = jnp.concatenate([
        jnp.tile(q_norm_a.astype(F32) * (scale * LOG2E), N_HEADS_A), jnp.tile(k_norm_a.astype(F32), N_HEADS_A), jnp.tile(ones, N_HEADS_A),
        jnp.tile(q_norm_b.astype(F32) * (scale * LOG2E), N_HEADS_B), jnp.tile(k_norm_b.astype(F32), N_HEADS_B), jnp.tile(ones, N_HEADS_B),
    ]).reshape(1, QKV_COLS)
    flag = jnp.concatenate([
        jnp.ones((2 * WIDTH_A,), F32), jnp.zeros((WIDTH_A,), F32), jnp.ones((2 * WIDTH_B,), F32), jnp.zeros((WIDTH_B,), F32),
    ]).reshape(1, QKV_COLS)

    xn = _rmsnorm_bf16(h, norm1_gain)
    qkv = _qkv_proj(xn, w_in_t, gain, flag)
    c, ct = _forget_cumsum(xn, w_in_t, b_f)
    y_a = _dilated_mixture(qkv)
    y_b = _forgetting_attention(qkv, c, ct)
    merged, v_t, w_out_b = _gated_up(xn, y_a, y_b, w_gates, w_up_a.astype(BF16), w_up_b.astype(BF16), peer_v, w_out)
    h, u_b = _out_proj(merged, w_out_b, h, peer_u)

    hn = _rmsnorm_bf16(h, norm2_gain)
    wq_t = w_peer_q.T.astype(BF16)
    sk = peer_subkeys.reshape(PEER_HEADS * 2 * N_KEYS, PEER_QDIM // 2).astype(BF16)
    sc_t = _peer_scores(hn, wq_t, sk)
    e1, w2, tau = _peer_select(sc_t)
    return _peer_dense(hn, u_b, v_t, sc_t, e1, w2, tau, h)


def kernel(x, norm1_gain, w_in, b_forget, q_norm_a, k_norm_a, q_norm_b, k_norm_b,
           w_up_a, w_up_b, w_out, norm2_gain, w_peer_q, peer_subkeys, peer_u, peer_v):
    b, s, d = x.shape
    assert b == 1 and d == D_MODEL and s % (DILATION_GROUPS[-1][0]) == 0
    h = x.reshape(s, d)
    for layer in range(norm1_gain.shape[0]):
        h = _layer(h, norm1_gain[layer], w_in[layer], b_forget[layer], q_norm_a[layer], k_norm_a[layer],
                   q_norm_b[layer], k_norm_b[layer], w_up_a[layer], w_up_b[layer], w_out[layer], norm2_gain[layer],
                   w_peer_q[layer], peer_subkeys[layer], peer_u[layer], peer_v[layer])
    return h.reshape(b, s, d)
```
